```python
import jax, jax.numpy as jnp
from jax import lax
import numpy as np

D_MODEL = 1024
BATCH = 8
SEQ = 8192
DEPTH = 4

CHUNK = 64
Q_BLOCK = 128
N_MIXERS = 2
N_MLA_LAYERS = (DEPTH + 1) // 2
N_GDN_LAYERS = DEPTH // 2
MLA_HEADS = D_MODEL // 128
MLA_NOPE = 128
MLA_ROPE = 64
MLA_V = 128
MLA_Q_LORA = 3 * D_MODEL // 8
MLA_KV_LORA = D_MODEL // 4
ROPE_THETA = 10000.0
GDN_HEADS = D_MODEL // 128
GDN_DK = 128
GDN_DV = 128
GDN_CONV = 4
D_FF = 4 * D_MODEL
NORM_EPS = 1e-6
MAX_POS_OFFSET = 65536

kernel_name = 'hybrid_mla_gdn_sandwich_adaln'


def rms_norm(x, g, eps=NORM_EPS):
    xf = x.astype(jnp.float32)
    y = xf * lax.rsqrt(jnp.mean(xf * xf, axis=-1, keepdims=True) + eps)
    return (y * g.astype(jnp.float32)).astype(x.dtype)


def l2_norm(x, eps=1e-6):
    xf = x.astype(jnp.float32)
    return xf * lax.rsqrt(jnp.sum(xf * xf, axis=-1, keepdims=True) + eps)


def rope_tables(positions):
    half = MLA_ROPE // 2
    inv_freq = ROPE_THETA ** (-jnp.arange(half, dtype=jnp.float32) / half)
    ang = positions.astype(jnp.float32)[..., None] * inv_freq
    return jnp.cos(ang), jnp.sin(ang)


def apply_rope(x, cos, sin):
    xf = x.astype(jnp.float32)
    x1, x2 = jnp.split(xf, 2, axis=-1)
    return jnp.concatenate([x1 * cos - x2 * sin, x2 * cos + x1 * sin], axis=-1).astype(x.dtype)


def mla_mixer(h, cos, sin, w_in, q_norm_g, kv_norm_g, w_uq, w_ukv, w_o):
    B, S, _ = h.shape
    H = MLA_HEADS
    proj = h @ w_in
    c_q, c_kv, k_rope = jnp.split(proj, [MLA_Q_LORA, MLA_Q_LORA + MLA_KV_LORA], axis=-1)
    q = (rms_norm(c_q, q_norm_g) @ w_uq).reshape(B, S, H, MLA_NOPE + MLA_ROPE)
    q_nope, q_rope = jnp.split(q, [MLA_NOPE], axis=-1)
    q_rope = apply_rope(q_rope, cos[:, :, None, :], sin[:, :, None, :])
    kv = (rms_norm(c_kv, kv_norm_g) @ w_ukv).reshape(B, S, H, MLA_NOPE + MLA_V)
    k_nope, v = jnp.split(kv, [MLA_NOPE], axis=-1)
    k_rope = apply_rope(k_rope, cos, sin)
    scale = (MLA_NOPE + MLA_ROPE) ** -0.5
    nb = S // Q_BLOCK
    qn_blocks = q_nope.reshape(B, nb, Q_BLOCK, H, MLA_NOPE).transpose(1, 0, 2, 3, 4)
    qr_blocks = q_rope.reshape(B, nb, Q_BLOCK, H, MLA_ROPE).transpose(1, 0, 2, 3, 4)
    key_chunk = jnp.arange(S) // CHUNK
    neg = jnp.finfo(jnp.float32).min

    def attend(args):
        qn, qr, blk = args
        s = (jnp.einsum('bqhd,bkhd->bhqk', qn, k_nope)
             + jnp.einsum('bqhr,bkr->bhqk', qr, k_rope)).astype(jnp.float32) * scale
        q_chunk = (blk * Q_BLOCK + jnp.arange(Q_BLOCK)) // CHUNK
        mask = key_chunk[None, :] <= q_chunk[:, None]
        p = jax.nn.softmax(jnp.where(mask, s, neg), axis=-1).astype(v.dtype)
        return jnp.einsum('bhqk,bkhd->bqhd', p, v)

    o = lax.map(attend, (qn_blocks, qr_blocks, jnp.arange(nb)))
    o = o.transpose(1, 0, 2, 3, 4).reshape(B, S, H * MLA_V)
    return o @ w_o


def causal_depthwise_conv(x, w):
    K, C = w.shape
    return lax.conv_general_dilated(x, w[:, None, :].astype(x.dtype), window_strides=(1,),
                                    padding=[(K - 1, 0)],
                                    dimension_numbers=('NWC', 'WIO', 'NWC'),
                                    feature_group_count=C)


def gated_delta_rule_chunked(q, k, v, g, beta):
    B, S, H, Dk = q.shape
    Dv = v.shape[-1]
    N = S // CHUNK

    def chunks(t):
        return t.reshape(B, N, CHUNK, H, -1).transpose(1, 0, 3, 2, 4)

    q, k, v = chunks(q), chunks(k), chunks(v)
    beta = beta.reshape(B, N, CHUNK, H).transpose(1, 0, 3, 2)
    g = jnp.cumsum(g.reshape(B, N, CHUNK, H).transpose(1, 0, 3, 2), axis=-1)
    incl = jnp.tril(jnp.ones((CHUNK, CHUNK), dtype=bool))
    strict = jnp.tril(jnp.ones((CHUNK, CHUNK), dtype=bool), k=-1)
    decay = jnp.exp(jnp.where(incl, g[..., :, None] - g[..., None, :], -jnp.inf))
    kb = k * beta[..., None]
    L = jnp.where(strict, jnp.einsum('nbhcd,nbhjd->nbhcj', kb, k) * decay, 0.0)
    A = L + jnp.eye(CHUNK, dtype=L.dtype)
    rhs = jnp.concatenate([v * beta[..., None], kb * jnp.exp(g)[..., None]], axis=-1)
    sol = lax.linalg.triangular_solve(A, rhs, left_side=True, lower=True, unit_diagonal=True)
    u, w = jnp.split(sol, [Dv], axis=-1)
    a_qk = jnp.einsum('nbhcd,nbhjd->nbhcj', q, k) * decay
    q_dec = q * jnp.exp(g)[..., None]
    g_last = g[..., -1]
    k_dec = k * jnp.exp(g_last[..., None] - g)[..., None]

    def step(state, inp):
        u_n, w_n, qd_n, aqk_n, kd_n, gl_n = inp
        v_new = u_n - jnp.einsum('bhcd,bhde->bhce', w_n, state)
        o_n = jnp.einsum('bhcd,bhde->bhce', qd_n, state) + jnp.einsum('bhcj,bhje->bhce', aqk_n, v_new)
        state = state * jnp.exp(gl_n)[..., None, None] + jnp.einsum('bhcd,bhce->bhde', kd_n, v_new)
        return state, o_n

    s0 = jnp.zeros((B, H, Dk, Dv), jnp.float32)
    _, o = lax.scan(step, s0, (u, w, q_dec, a_qk, k_dec, g_last))
    return o.transpose(1, 0, 3, 2, 4).reshape(B, S, H, Dv)


def gdn_mixer(h, w_in, conv_w, a_log, dt_bias, out_norm_g, w_o):
    B, S, _ = h.shape
    H = GDN_HEADS
    f32 = jnp.float32
    qkv_w = H * (2 * GDN_DK + GDN_DV)
    proj = h @ w_in
    qkv, b_raw, a_raw, z = jnp.split(proj, [qkv_w, qkv_w + H, qkv_w + 2 * H], axis=-1)
    qkv = jax.nn.silu(causal_depthwise_conv(qkv, conv_w))
    q, k, v = jnp.split(qkv, [H * GDN_DK, 2 * H * GDN_DK], axis=-1)
    q = l2_norm(q.reshape(B, S, H, GDN_DK)) * (GDN_DK ** -0.5)
    k = l2_norm(k.reshape(B, S, H, GDN_DK))
    v = v.reshape(B, S, H, GDN_DV).astype(f32)
    beta = jax.nn.sigmoid(b_raw.astype(f32))
    g = -jnp.exp(a_log.astype(f32)) * jax.nn.softplus(a_raw.astype(f32) + dt_bias.astype(f32))
    o = gated_delta_rule_chunked(q, k, v, g, beta)
    o = rms_norm(o, out_norm_g) * jax.nn.silu(z.astype(f32)).reshape(B, S, H, GDN_DV)
    return o.reshape(B, S, H * GDN_DV).astype(h.dtype) @ w_o


def setup_inputs(seed: int = 0) -> dict:
    key = jax.random.key(seed)
    ks = jax.random.split(key, 24)
    f32 = jnp.float32

    def dense(k, shape, fan_in, scale=1.0):
        return jax.random.normal(k, shape, f32) * (scale * fan_in ** -0.5)

    def gain(k, shape):
        return 1.0 + 0.05 * jax.random.normal(k, shape, f32)

    x = jax.random.normal(ks[0], (BATCH, SEQ, D_MODEL), f32)
    c = jax.random.normal(ks[1], (BATCH, D_MODEL), f32)
    offset = jax.random.randint(ks[2], (BATCH, 1), 0, MAX_POS_OFFSET, dtype=jnp.int32)
    positions = offset + jnp.arange(SEQ, dtype=jnp.int32)[None, :]
    ada_w = dense(ks[3], (DEPTH, D_MODEL, 6 * D_MODEL), D_MODEL, 0.5)
    ada_b = 0.02 * jax.random.normal(ks[4], (DEPTH, 6 * D_MODEL), f32)
    sandwich_g = gain(ks[5], (DEPTH, 4, D_MODEL))
    mla_w_in = dense(ks[6], (N_MLA_LAYERS, D_MODEL, MLA_Q_LORA + MLA_KV_LORA + MLA_ROPE), D_MODEL)
    mla_q_norm_g = gain(ks[7], (N_MLA_LAYERS, MLA_Q_LORA))
    mla_kv_norm_g = gain(ks[8], (N_MLA_LAYERS, MLA_KV_LORA))
    mla_w_uq = dense(ks[9], (N_MLA_LAYERS, MLA_Q_LORA, MLA_HEADS * (MLA_NOPE + MLA_ROPE)), MLA_Q_LORA)
    mla_w_ukv = dense(ks[10], (N_MLA_LAYERS, MLA_KV_LORA, MLA_HEADS * (MLA_NOPE + MLA_V)), MLA_KV_LORA)
    mla_w_o = dense(ks[11], (N_MLA_LAYERS, MLA_HEADS * MLA_V, D_MODEL), MLA_HEADS * MLA_V)
    qkv_w = GDN_HEADS * (2 * GDN_DK + GDN_DV)
    gdn_in_width = qkv_w + 2 * GDN_HEADS + GDN_HEADS * GDN_DV
    col_scale = jnp.ones((gdn_in_width,), f32).at[qkv_w + GDN_HEADS:qkv_w + 2 * GDN_HEADS].set(0.1)
    gdn_w_in = dense(ks[12], (N_GDN_LAYERS, D_MODEL, gdn_in_width), D_MODEL) * col_scale
    gdn_conv_w = dense(ks[13], (N_GDN_LAYERS, GDN_CONV, qkv_w), GDN_CONV)
    gdn_a_log = jnp.log(jax.random.uniform(ks[14], (N_GDN_LAYERS, GDN_HEADS), f32, 1.0, 16.0))
    dt = jnp.exp(jax.random.uniform(ks[15], (N_GDN_LAYERS, GDN_HEADS), f32,
                                    float(np.log(1e-3)), float(np.log(1e-1))))
    gdn_dt_bias = dt + jnp.log(-jnp.expm1(-dt))
    gdn_out_norm_g = gain(ks[16], (N_GDN_LAYERS, GDN_DV))
    gdn_w_o = dense(ks[17], (N_GDN_LAYERS, GDN_HEADS * GDN_DV, D_MODEL), GDN_HEADS * GDN_DV)
    ffn_w1 = dense(ks[18], (DEPTH, D_MODEL, D_FF), D_MODEL)
    ffn_w2 = dense(ks[19], (DEPTH, D_FF, D_MODEL), D_FF)
    return {'x': x, 'c': c, 'positions': positions, 'ada_w': ada_w, 'ada_b': ada_b,
            'sandwich_g': sandwich_g, 'mla_w_in': mla_w_in, 'mla_q_norm_g': mla_q_norm_g,
            'mla_kv_norm_g': mla_kv_norm_g, 'mla_w_uq': mla_w_uq, 'mla_w_ukv': mla_w_ukv,
            'mla_w_o': mla_w_o, 'gdn_w_in': gdn_w_in, 'gdn_conv_w': gdn_conv_w,
            'gdn_a_log': gdn_a_log, 'gdn_dt_bias': gdn_dt_bias, 'gdn_out_norm_g': gdn_out_norm_g,
            'gdn_w_o': gdn_w_o, 'ffn_w1': ffn_w1, 'ffn_w2': ffn_w2}


def reference(x, c, positions, ada_w, ada_b, sandwich_g, mla_w_in, mla_q_norm_g, mla_kv_norm_g,
              mla_w_uq, mla_w_ukv, mla_w_o, gdn_w_in, gdn_conv_w, gdn_a_log, gdn_dt_bias,
              gdn_out_norm_g, gdn_w_o, ffn_w1, ffn_w2):
    cos, sin = rope_tables(positions)
    c_act = jax.nn.silu(c)
    for i in range(DEPTH):
        mod = c_act @ ada_w[i] + ada_b[i]
        sh_m, sc_m, gt_m, sh_f, sc_f, gt_f = [m[:, None, :] for m in jnp.split(mod, 6, axis=-1)]
        h = rms_norm(x, sandwich_g[i, 0]) * (1 + sc_m) + sh_m
        j = i // N_MIXERS
        if i % N_MIXERS == 0:
            y = mla_mixer(h, cos, sin, mla_w_in[j], mla_q_norm_g[j], mla_kv_norm_g[j],
                          mla_w_uq[j], mla_w_ukv[j], mla_w_o[j])
        else:
            y = gdn_mixer(h, gdn_w_in[j], gdn_conv_w[j], gdn_a_log[j], gdn_dt_bias[j],
                          gdn_out_norm_g[j], gdn_w_o[j])
        x = x + gt_m * rms_norm(y, sandwich_g[i, 1])
        h = rms_norm(x, sandwich_g[i, 2]) * (1 + sc_f) + sh_f
        y = jnp.square(jax.nn.relu(h @ ffn_w1[i])) @ ffn_w2[i]
        x = x + gt_f * rms_norm(y, sandwich_g[i, 3])
    return x
```

```python
import functools
import math

import jax
import jax.numpy as jnp
from jax import lax
from jax.experimental import pallas as pl
from jax.experimental.pallas import tpu as pltpu

F32 = jnp.float32
BF16 = jnp.bfloat16

LANES = 128
CHUNK = 64
HEADS = 8
HEAD_DIM = 128
ROPE_DIM = 64
NORM_EPS = 1e-6
ROPE_THETA = 10000.0
CONV_TAPS = 4
VMEM_LIMIT = 56 * 1024 * 1024

TOKEN_TILE = 512
ATTN_TILE = 512
GDN_SEG = 256


def _rms(x):
    return x * lax.rsqrt(jnp.mean(x * x, axis=-1, keepdims=True) + NORM_EPS)


def _dot(a, b):
    return jnp.dot(a, b, preferred_element_type=F32)


def _dot_nt(a, b):
    return lax.dot_general(a, b, (((1,), (1,)), ((), ())), preferred_element_type=F32)


def _dot_tn(a, b):
    return lax.dot_general(a, b, (((0,), (0,)), ((), ())), preferred_element_type=F32)


def _params(*sem):
    return pltpu.CompilerParams(dimension_semantics=sem, vmem_limit_bytes=VMEM_LIMIT)


def _resident(shape):
    return pl.BlockSpec(shape, lambda *_: (0,) * len(shape), pipeline_mode=pl.Buffered(1))


def _adaln_kernel(c_ref, w_ref, b_ref, o_ref):
    c = c_ref[...]
    act = (c * jax.nn.sigmoid(c)).astype(BF16)
    o_ref[0] = _dot(act, w_ref[0].astype(BF16)) + b_ref[0]


def _adaln(c, ada_w, ada_b):
    depth, d, n = ada_w.shape
    b = c.shape[0]
    tn = n // 4
    return pl.pallas_call(
        _adaln_kernel,
        out_shape=jax.ShapeDtypeStruct((depth, b, n), F32),
        grid=(depth, n // tn),
        in_specs=[pl.BlockSpec((b, d), lambda i, j: (0, 0)),
                  pl.BlockSpec((1, d, tn), lambda i, j: (i, 0, j)),
                  pl.BlockSpec((1, 1, tn), lambda i, j: (i, 0, j))],
        out_specs=pl.BlockSpec((1, b, tn), lambda i, j: (i, 0, j)),
        compiler_params=_params("arbitrary", "arbitrary"),
        name="adaln_mod",
    )(c, ada_w, ada_b.reshape(depth, 1, n))


def _rope_kernel(pos_ref, inv_ref, cos_ref, sin_ref):
    ang = pos_ref[...].astype(F32) * inv_ref[...]
    cos_ref[...] = jnp.cos(ang)
    sin_ref[...] = jnp.sin(ang)


def _rope_tables(positions):
    t = positions.size
    half = ROPE_DIM // 2
    inv_freq = ROPE_THETA ** (-jnp.arange(half, dtype=F32) / half)
    inv_row = jnp.tile(inv_freq, LANES // half).reshape(1, LANES)
    tm = 1024
    return pl.pallas_call(
        _rope_kernel,
        out_shape=(jax.ShapeDtypeStruct((t, LANES), F32),) * 2,
        grid=(t // tm,),
        in_specs=[pl.BlockSpec((tm, 1), lambda i: (i, 0)),
                  pl.BlockSpec((1, LANES), lambda i: (0, 0))],
        out_specs=(pl.BlockSpec((tm, LANES), lambda i: (i, 0)),) * 2,
        compiler_params=_params("arbitrary"),
        name="rope_tables",
    )(positions.reshape(t, 1), inv_row)


def _mla_pre_kernel(x_ref, mod_ref, g_ref, cos_ref, sin_ref, win_ref, gq_ref, gkv_ref,
                    wuq_ref, wukv_ref, q_ref, k_ref, v_ref, *, dq, dkv, scale):
    d_nope = HEADS * HEAD_DIM
    mod = mod_ref[0]
    h = _rms(x_ref[...]) * g_ref[...] * (1.0 + mod[1:2]) + mod[0:1]
    proj = _dot(h.astype(BF16), win_ref[...])
    cos = cos_ref[...]
    sin = sin_ref[...]
    k_rope = (proj[:, dq + dkv:dq + dkv + LANES] * cos
              + proj[:, dq + dkv + LANES:dq + dkv + 2 * LANES] * sin).astype(BF16)
    qn = (_rms(proj[:, :dq]) * gq_ref[...]).astype(BF16)
    qf = _dot(qn, wuq_ref[...])
    cos_h = jnp.concatenate([cos] * HEADS, axis=1)
    sin_h = jnp.concatenate([sin] * HEADS, axis=1)
    q_nope = (qf[:, :d_nope] * scale).astype(BF16)
    q_rope = ((qf[:, d_nope:2 * d_nope] * cos_h + qf[:, 2 * d_nope:] * sin_h) * scale).astype(BF16)
    kvn = (_rms(proj[:, dq:dq + dkv]) * gkv_ref[...]).astype(BF16)
    kv = _dot(kvn, wukv_ref[...])
    k_nope = kv[:, :d_nope].astype(BF16)
    v_ref[...] = kv[:, d_nope:].astype(BF16)
    for hd in range(HEADS):
        lo = hd * HEAD_DIM
        q_ref[:, 2 * lo:2 * lo + HEAD_DIM] = q_nope[:, lo:lo + HEAD_DIM]
        q_ref[:, 2 * lo + HEAD_DIM:2 * lo + 2 * HEAD_DIM] = q_rope[:, lo:lo + HEAD_DIM]
        k_ref[:, 2 * lo:2 * lo + HEAD_DIM] = k_nope[:, lo:lo + HEAD_DIM]
        k_ref[:, 2 * lo + HEAD_DIM:2 * lo + 2 * HEAD_DIM] = k_rope


def _swap_rot(w):
    half = ROPE_DIM // 2
    return jnp.concatenate([-w[..., half:], w[..., :half]], axis=-1)


def _mla_weights(w_in, w_uq, w_ukv, dq, dkv):
    d = w_in.shape[0]
    wkr = w_in[:, dq + dkv:]
    wkr_sw = _swap_rot(wkr)
    win = jnp.concatenate([w_in[:, :dq + dkv], wkr, wkr, wkr_sw, wkr_sw], axis=1).astype(BF16)
    wq = w_uq.reshape(dq, HEADS, HEAD_DIM + ROPE_DIM)
    nope = wq[:, :, :HEAD_DIM].reshape(dq, HEADS * HEAD_DIM)
    rope = wq[:, :, HEAD_DIM:]
    pad = jnp.zeros((dq, HEADS, HEAD_DIM - ROPE_DIM), F32)
    rope_p = jnp.concatenate([rope, pad], axis=-1).reshape(dq, HEADS * HEAD_DIM)
    rope_sw = jnp.concatenate([_swap_rot(rope), pad], axis=-1).reshape(dq, HEADS * HEAD_DIM)
    wuq = jnp.concatenate([nope, rope_p, rope_sw], axis=1).astype(BF16)
    wkv = w_ukv.reshape(dkv, HEADS, 2, HEAD_DIM).transpose(0, 2, 1, 3).reshape(dkv, 2 * HEADS * HEAD_DIM)
    del d
    return win, wuq, wkv.astype(BF16)


def _mla_pre(x, mod, g, cos, sin, win, gq, gkv, wuq, wukv, seq):
    t, d = x.shape
    dq, dkv = gq.shape[-1], gkv.shape[-1]
    tm = TOKEN_TILE
    per_seq = seq // tm
    d_nope = HEADS * HEAD_DIM
    scale = float((HEAD_DIM + ROPE_DIM) ** -0.5)
    row = lambda i: (i, 0)
    return pl.pallas_call(
        functools.partial(_mla_pre_kernel, dq=dq, dkv=dkv, scale=scale),
        out_shape=(jax.ShapeDtypeStruct((t, 2 * d_nope), BF16),
                   jax.ShapeDtypeStruct((t, 2 * d_nope), BF16),
                   jax.ShapeDtypeStruct((t, d_nope), BF16)),
        grid=(t // tm,),
        in_specs=[pl.BlockSpec((tm, d), row),
                  pl.BlockSpec((1, 6, d), lambda i: (i // per_seq, 0, 0)),
                  _resident((1, d)),
                  pl.BlockSpec((tm, LANES), row),
                  pl.BlockSpec((tm, LANES), row),
                  _resident(win.shape), _resident((1, dq)), _resident((1, dkv)),
                  _resident(wuq.shape), _resident(wukv.shape)],
        out_specs=(pl.BlockSpec((tm, 2 * d_nope), row),
                   pl.BlockSpec((tm, 2 * d_nope), row),
                   pl.BlockSpec((tm, d_nope), row)),
        compiler_params=_params("arbitrary"),
        name="mla_pre",
    )(x, mod, g.reshape(1, d), cos, sin, win, gq.reshape(1, dq), gkv.reshape(1, dkv), wuq, wukv)


def _attn_kernel(q_ref, k_ref, v_ref, o_ref, *, tile):
    i = pl.program_id(2)
    q = q_ref[...]

    def step(j, carry, masked):
        m, l, acc = carry
        start = pl.multiple_of(j * tile, tile)
        s = _dot_nt(q, k_ref[pl.ds(start, tile), :])
        if masked:
            qc = lax.broadcasted_iota(jnp.int32, (tile, tile), 0) // CHUNK
            kc = lax.broadcasted_iota(jnp.int32, (tile, tile), 1) // CHUNK
            s = jnp.where(kc <= qc, s, -jnp.inf)
        m_new = jnp.maximum(m, jnp.max(s, axis=-1, keepdims=True))
        p = jnp.exp(s - m_new)
        alpha = jnp.exp(m - m_new)
        l = alpha * l + jnp.sum(p, axis=-1, keepdims=True)
        acc = alpha * acc + _dot(p.astype(BF16), v_ref[pl.ds(start, tile), :])
        return m_new, l, acc

    init = (jnp.full((tile, 1), -jnp.inf, F32), jnp.zeros((tile, 1), F32),
            jnp.zeros((tile, HEAD_DIM), F32))
    carry = lax.fori_loop(0, i, lambda j, c: step(j, c, False), init)
    _, l, acc = step(i, carry, True)
    o_ref[...] = (acc / l).astype(o_ref.dtype)


def _attention(q, k, v, batch, seq):
    t = q.shape[0]
    tile = ATTN_TILE
    nq = seq // tile
    return pl.pallas_call(
        functools.partial(_attn_kernel, tile=tile),
        out_shape=jax.ShapeDtypeStruct((t, HEADS * HEAD_DIM), BF16),
        grid=(batch, HEADS, nq),
        in_specs=[pl.BlockSpec((tile, 2 * HEAD_DIM), lambda b, h, i: (b * nq + i, h)),
                  pl.BlockSpec((seq, 2 * HEAD_DIM), lambda b, h, i: (b, h)),
                  pl.BlockSpec((seq, HEAD_DIM), lambda b, h, i: (b, h))],
        out_specs=pl.BlockSpec((tile, HEAD_DIM), lambda b, h, i: (b * nq + i, h)),
        compiler_params=_params("arbitrary", "arbitrary", "arbitrary"),
        name="mla_attention",
    )(q, k, v)


def _post_kernel(o_ref, x_ref, mod_ref, g_ref, wo_ref, w1_ref, w2_ref, out_ref, h1_ref, *, ff_tile):
    mod = mod_ref[0]
    g = g_ref[...]
    y = _dot(o_ref[...], wo_ref[...])
    x1 = x_ref[...] + mod[2:3] * (_rms(y) * g[1:2])
    h = (_rms(x1) * g[2:3] * (1.0 + mod[4:5]) + mod[3:4]).astype(BF16)
    d_ff = w1_ref.shape[1]
    for j in range(d_ff // ff_tile):
        a = jnp.maximum(_dot(h, w1_ref[:, j * ff_tile:(j + 1) * ff_tile]), 0.0)
        h1_ref[:, j * ff_tile:(j + 1) * ff_tile] = (a * a).astype(BF16)
    y2 = _dot(h1_ref[...], w2_ref[...])
    out_ref[...] = x1 + mod[5:6] * (_rms(y2) * g[3:4])


def _post(o, x, mod, g4, wo, w1, w2, seq):
    t, d = x.shape
    tm = TOKEN_TILE
    per_seq = seq // tm
    d_ff = w1.shape[1]
    row = lambda i: (i, 0)
    return pl.pallas_call(
        functools.partial(_post_kernel, ff_tile=1024),
        out_shape=jax.ShapeDtypeStruct((t, d), F32),
        grid=(t // tm,),
        in_specs=[pl.BlockSpec((tm, o.shape[1]), row),
                  pl.BlockSpec((tm, d), row),
                  pl.BlockSpec((1, 6, d), lambda i: (i // per_seq, 0, 0)),
                  _resident((4, d)),
                  _resident(wo.shape), _resident(w1.shape), _resident(w2.shape)],
        out_specs=pl.BlockSpec((tm, d), row),
        scratch_shapes=[pltpu.VMEM((tm, d_ff), BF16)],
        input_output_aliases={1: 0},
        compiler_params=_params("arbitrary"),
        name="post_ffn",
    )(o, x, mod, g4, wo, w1, w2)


def _gdn_pre_kernel(x_ref, mod_ref, g_ref, win_ref, conv_ref, gate_ref,
                    q_ref, k_ref, v_ref, z_ref, gates_ref, buf_ref, *, per_seq):
    tm = x_ref.shape[0]
    d_qkv = 3 * HEADS * HEAD_DIM
    d_head = HEADS * HEAD_DIM
    halo = 8
    mod = mod_ref[0]
    h = _rms(x_ref[...]) * g_ref[...] * (1.0 + mod[1:2]) + mod[0:1]
    proj = _dot(h.astype(BF16), win_ref[...])

    @pl.when(pl.program_id(0) % per_seq == 0)
    def _():
        buf_ref[0:halo, :] = jnp.zeros((halo, d_qkv), F32)

    buf_ref[halo:halo + tm, :] = proj[:, :d_qkv]
    conv = conv_ref[0:1, :] * buf_ref[halo - 3:halo - 3 + tm, :]
    for tap in range(1, CONV_TAPS):
        conv = conv + conv_ref[tap:tap + 1, :] * buf_ref[halo - 3 + tap:halo - 3 + tap + tm, :]
    buf_ref[0:halo, :] = buf_ref[tm:tm + halo, :]
    act = conv * jax.nn.sigmoid(conv)

    for hd in range(HEADS):
        lo = hd * HEAD_DIM
        qh = act[:, lo:lo + HEAD_DIM]
        kh = act[:, d_head + lo:d_head + lo + HEAD_DIM]
        qn = qh * lax.rsqrt(jnp.sum(qh * qh, axis=-1, keepdims=True) + 1e-6)
        q_ref[:, lo:lo + HEAD_DIM] = (qn * (HEAD_DIM ** -0.5)).astype(BF16)
        k_ref[:, lo:lo + HEAD_DIM] = (
            kh * lax.rsqrt(jnp.sum(kh * kh, axis=-1, keepdims=True) + 1e-6)).astype(BF16)
    v_ref[...] = act[:, 2 * d_head:].astype(BF16)
    z = proj[:, d_qkv:d_qkv + d_head]
    z_ref[...] = (z * jax.nn.sigmoid(z)).astype(BF16)

    raw = proj[:, d_qkv + d_head:]
    gp = gate_ref[...]
    beta = jax.nn.sigmoid(raw)
    gl = gp[0:1] * jax.nn.softplus(raw + gp[1:2])
    pos = lax.broadcasted_iota(jnp.int32, (tm, LANES), 0) % CHUNK
    shift = 1
    while shift < CHUNK:
        gl = gl + jnp.where(pos >= shift, pltpu.roll(gl, shift, 0), 0.0)
        shift *= 2
    lane = lax.broadcasted_iota(jnp.int32, (tm, LANES), 1)
    gates_ref[...] = jnp.where(lane < HEADS, beta, gl)


def _gdn_weights(w_in, a_log, dt_bias):
    d_head = HEADS * HEAD_DIM
    d_qkv = 3 * d_head
    pad = jnp.zeros((w_in.shape[0], LANES - 2 * HEADS), F32)
    win = jnp.concatenate([w_in[:, :d_qkv], w_in[:, d_qkv + 2 * HEADS:],
                           w_in[:, d_qkv:d_qkv + 2 * HEADS], pad], axis=1).astype(BF16)
    zeros = jnp.zeros((HEADS,), F32)
    lane_pad = jnp.zeros((LANES - 2 * HEADS,), F32)
    neg_a = jnp.concatenate([zeros, -jnp.exp(a_log.astype(F32)), lane_pad])
    dtb = jnp.concatenate([zeros, dt_bias.astype(F32), lane_pad])
    return win, jnp.stack([neg_a, dtb])


def _gdn_pre(x, mod, g, win, conv_w, gate_p, seq):
    t, d = x.shape
    tm = TOKEN_TILE
    per_seq = seq // tm
    d_head = HEADS * HEAD_DIM
    row = lambda i: (i, 0)
    tok = jax.ShapeDtypeStruct((t, d_head), BF16)
    return pl.pallas_call(
        functools.partial(_gdn_pre_kernel, per_seq=per_seq),
        out_shape=(tok, tok, tok, tok, jax.ShapeDtypeStruct((t, LANES), F32)),
        grid=(t // tm,),
        in_specs=[pl.BlockSpec((tm, d), row),
                  pl.BlockSpec((1, 6, d), lambda i: (i // per_seq, 0, 0)),
                  _resident((1, d)), _resident(win.shape), _resident(conv_w.shape),
                  _resident((2, LANES))],
        out_specs=(pl.BlockSpec((tm, d_head), row),) * 4 + (pl.BlockSpec((tm, LANES), row),),
        scratch_shapes=[pltpu.VMEM((tm + 8, 3 * d_head), F32)],
        compiler_params=_params("arbitrary"),
        name="gdn_pre",
    )(x, mod, g.reshape(1, d), win, conv_w, gate_p)


def _split(a):
    hi = a.astype(BF16)
    return hi, (a - hi.astype(F32)).astype(BF16)


def _dot_x3(a, b):
    ah, al = _split(a)
    bh, bl = _split(b)
    return _dot(ah, bh) + (_dot(ah, bl) + _dot(al, bh))


def _unit_lower_inverse(low, eye):
    inv = eye - low
    power = low
    order = 2
    while order < CHUNK:
        power = _dot_x3(power, power)
        inv = inv + _dot_x3(inv, power)
        order *= 2
    return inv


def _gdn_core_kernel(q_ref, k_ref, v_ref, z_ref, gates_ref, gt_ref, gout_ref, o_ref, state_ref):
    seg = q_ref.shape[0]
    n_chunks = seg // CHUNK

    @pl.when(pl.program_id(1) == 0)
    def _():
        state_ref[...] = jnp.zeros(state_ref.shape, F32)

    row = lax.broadcasted_iota(jnp.int32, (seg, seg), 0)
    col = lax.broadcasted_iota(jnp.int32, (seg, seg), 1)
    same = (row // CHUNK) == (col // CHUNK)
    lower = jnp.where(same & (row >= col), 1.0, 0.0)
    strict = jnp.where(same & (row > col), 1.0, 0.0)
    eye = jnp.where(row == col, 1.0, 0.0)
    gout = gout_ref[...]

    for hd in range(HEADS):
        cols = slice(hd * HEAD_DIM, (hd + 1) * HEAD_DIM)
        k_b = k_ref[:, cols]
        q_b = q_ref[:, cols]
        kf = k_b.astype(F32)
        beta = gates_ref[:, hd:hd + 1]
        g_col = gates_ref[:, HEADS + hd:HEADS + hd + 1]
        g_row = gt_ref[HEADS + hd:HEADS + hd + 1, :]
        decay = jnp.exp(jnp.minimum(g_col - g_row, 0.0))
        kb = kf * beta
        low = _dot_nt(kb.astype(BF16), k_b) * (decay * strict)
        a_qk = (_dot_nt(q_b, k_b) * (decay * lower)).astype(BF16)
        inv = _unit_lower_inverse(low, eye)
        e_col = jnp.exp(g_col)
        rhs = jnp.concatenate([v_ref[:, cols].astype(F32) * beta, kb * e_col], axis=1)
        uw = _dot_x3(inv, rhs)
        q_dec = q_b.astype(F32) * e_col

        state = state_ref[hd]
        for n in range(n_chunks):
            rows = slice(n * CHUNK, (n + 1) * CHUNK)
            g_last = g_row[:, (n + 1) * CHUNK - 1:(n + 1) * CHUNK]
            lhs = jnp.concatenate([uw[rows, HEAD_DIM:], q_dec[rows]], axis=0).astype(BF16)
            ws = _dot(lhs, state.astype(BF16))
            v_new = uw[rows, :HEAD_DIM] - ws[:CHUNK]
            v_nb = v_new.astype(BF16)
            o = ws[CHUNK:] + _dot(a_qk[rows, rows], v_nb)
            k_dec = (kf[rows] * jnp.exp(g_last - g_col[rows])).astype(BF16)
            state = state * jnp.exp(g_last) + _dot_tn(k_dec, v_nb)
            o = _rms(o) * gout * z_ref[rows, cols].astype(F32)
            o_ref[rows, cols] = o.astype(o_ref.dtype)
        state_ref[hd] = state


def _gdn_core(q, k, v, z, gates, gates_t, gout, batch, seq):
    t, d_head = q.shape
    seg = GDN_SEG
    per_seq = seq // seg
    blk = pl.BlockSpec((seg, d_head), lambda b, s: (b * per_seq + s, 0))
    return pl.pallas_call(
        _gdn_core_kernel,
        out_shape=jax.ShapeDtypeStruct((t, d_head), BF16),
        grid=(batch, per_seq),
        in_specs=[blk, blk, blk, blk,
                  pl.BlockSpec((seg, LANES), lambda b, s: (b * per_seq + s, 0)),
                  pl.BlockSpec((2 * HEADS, seg), lambda b, s: (0, b * per_seq + s)),
                  pl.BlockSpec((1, HEAD_DIM), lambda b, s: (0, 0))],
        out_specs=blk,
        scratch_shapes=[pltpu.VMEM((HEADS, HEAD_DIM, HEAD_DIM), F32)],
        compiler_params=_params("arbitrary", "arbitrary"),
        name="gdn_core",
    )(q, k, v, z, gates, gates_t, gout.reshape(1, HEAD_DIM))


def kernel(x, c, positions, ada_w, ada_b, sandwich_g, mla_w_in, mla_q_norm_g, mla_kv_norm_g, mla_w_uq, mla_w_ukv, mla_w_o, gdn_w_in, gdn_conv_w, gdn_a_log, gdn_dt_bias, gdn_out_norm_g, gdn_w_o, ffn_w1, ffn_w2):
    batch, seq, d = x.shape
    depth = ada_w.shape[0]
    t = batch * seq
    assert seq % TOKEN_TILE == 0 and seq % ATTN_TILE == 0 and seq % GDN_SEG == 0
    mod = _adaln(c, ada_w, ada_b).reshape(depth, batch, 6, d)
    cos, sin = _rope_tables(positions)
    xs = x.reshape(t, d)
    for i in range(depth):
        j = i // 2
        if i % 2 == 0:
            dq, dkv = mla_q_norm_g.shape[-1], mla_kv_norm_g.shape[-1]
            win, wuq, wukv = _mla_weights(mla_w_in[j], mla_w_uq[j], mla_w_ukv[j], dq, dkv)
            q, k, v = _mla_pre(xs, mod[i], sandwich_g[i, 0], cos, sin, win, mla_q_norm_g[j],
                               mla_kv_norm_g[j], wuq, wukv, seq)
            o = _attention(q, k, v, batch, seq)
            wo = mla_w_o[j]
        else:
            win, gate_p = _gdn_weights(gdn_w_in[j], gdn_a_log[j], gdn_dt_bias[j])
            q, k, v, z, gates = _gdn_pre(xs, mod[i], sandwich_g[i, 0], win, gdn_conv_w[j], gate_p, seq)
            gates_t = gates[:, :2 * HEADS].T
            o = _gdn_core(q, k, v, z, gates, gates_t, gdn_out_norm_g[j], batch, seq)
            wo = gdn_w_o[j]
        xs = _post(o, xs, mod[i], sandwich_g[i], wo.astype(BF16), ffn_w1[i].astype(BF16),
                   ffn_w2[i].astype(BF16), seq)
    return xs.reshape(batch, seq, d)
```

```python
import functools
import math

import jax
import jax.numpy as jnp
from jax import lax
from jax.experimental import pallas as pl
from jax.experimental.pallas import tpu as pltpu

F32 = jnp.float32
BF16 = jnp.bfloat16

LANES = 128
CHUNK = 64
HEADS = 8
HEAD_DIM = 128
ROPE_DIM = 64
NORM_EPS = 1e-6
ROPE_THETA = 10000.0
CONV_TAPS = 4
VMEM_LIMIT = 56 * 1024 * 1024

TOKEN_TILE = 512
ATTN_TILE = 1024
ATTN_ROWS = 256
GDN_SEG = 256


def _rms(x):
    return x * lax.rsqrt(jnp.mean(x * x, axis=-1, keepdims=True) + NORM_EPS)


def _dot(a, b):
    return jnp.dot(a, b, preferred_element_type=F32)


def _dot_nt(a, b):
    return lax.dot_general(a, b, (((1,), (1,)), ((), ())), preferred_element_type=F32)


def _dot_tn(a, b):
    return lax.dot_general(a, b, (((0,), (0,)), ((), ())), preferred_element_type=F32)


def _params(*sem):
    return pltpu.CompilerParams(dimension_semantics=sem, vmem_limit_bytes=VMEM_LIMIT)


def _resident(shape):
    return pl.BlockSpec(shape, lambda *_: (0,) * len(shape), pipeline_mode=pl.Buffered(1))


def _adaln_kernel(c_ref, w_ref, b_ref, o_ref):
    c = c_ref[...]
    act = (c * jax.nn.sigmoid(c)).astype(BF16)
    o_ref[0] = _dot(act, w_ref[0].astype(BF16)) + b_ref[0]


def _adaln(c, ada_w, ada_b):
    depth, d, n = ada_w.shape
    b = c.shape[0]
    tn = n // 4
    return pl.pallas_call(
        _adaln_kernel,
        out_shape=jax.ShapeDtypeStruct((depth, b, n), F32),
        grid=(depth, n // tn),
        in_specs=[pl.BlockSpec((b, d), lambda i, j: (0, 0)),
                  pl.BlockSpec((1, d, tn), lambda i, j: (i, 0, j)),
                  pl.BlockSpec((1, 1, tn), lambda i, j: (i, 0, j))],
        out_specs=pl.BlockSpec((1, b, tn), lambda i, j: (i, 0, j)),
        compiler_params=_params("arbitrary", "arbitrary"),
        name="adaln_mod",
    )(c, ada_w, ada_b.reshape(depth, 1, n))


def _rope_kernel(pos_ref, inv_ref, cos_ref, sin_ref):
    ang = pos_ref[...].astype(F32) * inv_ref[...]
    cos_ref[...] = jnp.cos(ang)
    sin_ref[...] = jnp.sin(ang)


def _rope_tables(positions):
    t = positions.size
    half = ROPE_DIM // 2
    inv_freq = ROPE_THETA ** (-jnp.arange(half, dtype=F32) / half)
    inv_row = jnp.tile(inv_freq, LANES // half).reshape(1, LANES)
    tm = 1024
    return pl.pallas_call(
        _rope_kernel,
        out_shape=(jax.ShapeDtypeStruct((t, LANES), F32),) * 2,
        grid=(t // tm,),
        in_specs=[pl.BlockSpec((tm, 1), lambda i: (i, 0)),
                  pl.BlockSpec((1, LANES), lambda i: (0, 0))],
        out_specs=(pl.BlockSpec((tm, LANES), lambda i: (i, 0)),) * 2,
        compiler_params=_params("arbitrary"),
        name="rope_tables",
    )(positions.reshape(t, 1), inv_row)


def _mla_pre_kernel(x_ref, mod_ref, g_ref, cos_ref, sin_ref, win_ref, gq_ref, gkv_ref,
                    wuq_ref, wukv_ref, q_ref, k_ref, v_ref, *, dq, dkv, scale):
    d_nope = HEADS * HEAD_DIM
    mod = mod_ref[0]
    h = _rms(x_ref[...]) * g_ref[...] * (1.0 + mod[1:2]) + mod[0:1]
    proj = _dot(h.astype(BF16), win_ref[...])
    cos = cos_ref[...]
    sin = sin_ref[...]
    k_rope = (proj[:, dq + dkv:dq + dkv + LANES] * cos
              + proj[:, dq + dkv + LANES:dq + dkv + 2 * LANES] * sin).astype(BF16)
    qn = (_rms(proj[:, :dq]) * gq_ref[...]).astype(BF16)
    qf = _dot(qn, wuq_ref[...])
    cos_h = jnp.concatenate([cos] * HEADS, axis=1)
    sin_h = jnp.concatenate([sin] * HEADS, axis=1)
    q_nope = (qf[:, :d_nope] * scale).astype(BF16)
    q_rope = ((qf[:, d_nope:2 * d_nope] * cos_h + qf[:, 2 * d_nope:] * sin_h) * scale).astype(BF16)
    kvn = (_rms(proj[:, dq:dq + dkv]) * gkv_ref[...]).astype(BF16)
    kv = _dot(kvn, wukv_ref[...])
    k_nope = kv[:, :d_nope].astype(BF16)
    v = kv[:, d_nope:].astype(BF16)
    ones = jnp.ones((v.shape[0], HEAD_DIM), BF16)
    for hd in range(HEADS):
        lo = hd * HEAD_DIM
        q_ref[:, 2 * lo:2 * lo + HEAD_DIM] = q_nope[:, lo:lo + HEAD_DIM]
        q_ref[:, 2 * lo + HEAD_DIM:2 * lo + 2 * HEAD_DIM] = q_rope[:, lo:lo + HEAD_DIM]
        k_ref[:, 2 * lo:2 * lo + HEAD_DIM] = k_nope[:, lo:lo + HEAD_DIM]
        k_ref[:, 2 * lo + HEAD_DIM:2 * lo + 2 * HEAD_DIM] = k_rope
        v_ref[:, 2 * lo:2 * lo + HEAD_DIM] = v[:, lo:lo + HEAD_DIM]
        v_ref[:, 2 * lo + HEAD_DIM:2 * lo + 2 * HEAD_DIM] = ones


def _swap_rot(w):
    half = ROPE_DIM // 2
    return jnp.concatenate([-w[..., half:], w[..., :half]], axis=-1)


def _mla_weights(w_in, w_uq, w_ukv, dq, dkv):
    d = w_in.shape[0]
    wkr = w_in[:, dq + dkv:]
    wkr_sw = _swap_rot(wkr)
    win = jnp.concatenate([w_in[:, :dq + dkv], wkr, wkr, wkr_sw, wkr_sw], axis=1).astype(BF16)
    wq = w_uq.reshape(dq, HEADS, HEAD_DIM + ROPE_DIM)
    nope = wq[:, :, :HEAD_DIM].reshape(dq, HEADS * HEAD_DIM)
    rope = wq[:, :, HEAD_DIM:]
    pad = jnp.zeros((dq, HEADS, HEAD_DIM - ROPE_DIM), F32)
    rope_p = jnp.concatenate([rope, pad], axis=-1).reshape(dq, HEADS * HEAD_DIM)
    rope_sw = jnp.concatenate([_swap_rot(rope), pad], axis=-1).reshape(dq, HEADS * HEAD_DIM)
    wuq = jnp.concatenate([nope, rope_p, rope_sw], axis=1).astype(BF16)
    wkv = w_ukv.reshape(dkv, HEADS, 2, HEAD_DIM).transpose(0, 2, 1, 3).reshape(dkv, 2 * HEADS * HEAD_DIM)
    del d
    return win, wuq, wkv.astype(BF16)


def _mla_pre(x, mod, g, cos, sin, win, gq, gkv, wuq, wukv, seq):
    t, d = x.shape
    dq, dkv = gq.shape[-1], gkv.shape[-1]
    tm = TOKEN_TILE
    per_seq = seq // tm
    d_nope = HEADS * HEAD_DIM
    scale = float((HEAD_DIM + ROPE_DIM) ** -0.5 * math.log2(math.e))
    row = lambda i: (i, 0)
    return pl.pallas_call(
        functools.partial(_mla_pre_kernel, dq=dq, dkv=dkv, scale=scale),
        out_shape=(jax.ShapeDtypeStruct((t, 2 * d_nope), BF16),
                   jax.ShapeDtypeStruct((t, 2 * d_nope), BF16),
                   jax.ShapeDtypeStruct((t, 2 * d_nope), BF16)),
        grid=(t // tm,),
        in_specs=[pl.BlockSpec((tm, d), row),
                  pl.BlockSpec((1, 6, d), lambda i: (i // per_seq, 0, 0)),
                  _resident((1, d)),
                  pl.BlockSpec((tm, LANES), row),
                  pl.BlockSpec((tm, LANES), row),
                  _resident(win.shape), _resident((1, dq)), _resident((1, dkv)),
                  _resident(wuq.shape), _resident(wukv.shape)],
        out_specs=(pl.BlockSpec((tm, 2 * d_nope), row),
                   pl.BlockSpec((tm, 2 * d_nope), row),
                   pl.BlockSpec((tm, 2 * d_nope), row)),
        compiler_params=_params("arbitrary"),
        name="mla_pre",
    )(x, mod, g.reshape(1, d), cos, sin, win, gq.reshape(1, dq), gkv.reshape(1, dkv), wuq, wukv)


def _attn_kernel(q_ref, k_ref, v_ref, o_ref, m_ref, acc_ref, bias_ref, *, tile):
    i = pl.program_id(2)

    @pl.when((pl.program_id(0) == 0) & (pl.program_id(1) == 0) & (i == 0))
    def _():
        qc = lax.broadcasted_iota(jnp.int32, (tile, tile), 0) // CHUNK
        kc = lax.broadcasted_iota(jnp.int32, (tile, tile), 1) // CHUNK
        bias_ref[...] = jnp.where(kc <= qc, 0.0, -jnp.inf)

    m_ref[...] = jnp.full(m_ref.shape, -jnp.inf, F32)
    acc_ref[...] = jnp.zeros(acc_ref.shape, F32)

    def step(j, masked):
        start = pl.multiple_of(j * tile, tile)
        k = k_ref[pl.ds(start, tile), :]
        v = v_ref[pl.ds(start, tile), :]
        blocks = [slice(r * ATTN_ROWS, (r + 1) * ATTN_ROWS) for r in range(tile // ATTN_ROWS)]
        seen = [rows.stop if masked else tile for rows in blocks]
        scores = [_dot_nt(q_ref[rows, :], k[:n]) for rows, n in zip(blocks, seen)]
        for rows, n, s in zip(blocks, seen, scores):
            if masked:
                s = s + bias_ref[rows, :n]
            m_prev = m_ref[rows, :]
            m_new = jnp.maximum(m_prev, jnp.max(s, axis=-1, keepdims=True))
            p = jnp.exp2(s - m_new).astype(BF16)
            acc_ref[rows, :] = jnp.exp2(m_prev - m_new) * acc_ref[rows, :] + _dot(p, v[:n])
            m_ref[rows, :] = m_new

    def body(j, carry):
        step(j, False)
        return carry

    lax.fori_loop(0, i, body, 0)
    step(i, True)
    acc = acc_ref[...]
    o_ref[...] = (acc[:, :HEAD_DIM] / acc[:, HEAD_DIM:]).astype(o_ref.dtype)


def _attention(q, k, v, batch, seq):
    t = q.shape[0]
    tile = ATTN_TILE
    nq = seq // tile
    return pl.pallas_call(
        functools.partial(_attn_kernel, tile=tile),
        out_shape=jax.ShapeDtypeStruct((t, HEADS * HEAD_DIM), BF16),
        grid=(batch, HEADS, nq),
        in_specs=[pl.BlockSpec((tile, 2 * HEAD_DIM), lambda b, h, i: (b * nq + i, h)),
                  pl.BlockSpec((seq, 2 * HEAD_DIM), lambda b, h, i: (b, h)),
                  pl.BlockSpec((seq, 2 * HEAD_DIM), lambda b, h, i: (b, h))],
        out_specs=pl.BlockSpec((tile, HEAD_DIM), lambda b, h, i: (b * nq + i, h)),
        scratch_shapes=[pltpu.VMEM((tile, 1), F32), pltpu.VMEM((tile, 2 * HEAD_DIM), F32),
                        pltpu.VMEM((tile, tile), F32)],
        compiler_params=_params("arbitrary", "arbitrary", "arbitrary"),
        name="mla_attention",
    )(q, k, v)


def _post_kernel(o_ref, x_ref, mod_ref, g_ref, wo_ref, w1_ref, w2_ref, out_ref, h1_ref, *, ff_tile):
    mod = mod_ref[0]
    g = g_ref[...]
    y = _dot(o_ref[...], wo_ref[...])
    x1 = x_ref[...] + mod[2:3] * (_rms(y) * g[1:2])
    h = (_rms(x1) * g[2:3] * (1.0 + mod[4:5]) + mod[3:4]).astype(BF16)
    d_ff = w1_ref.shape[1]
    for j in range(d_ff // ff_tile):
        a = jnp.maximum(_dot(h, w1_ref[:, j * ff_tile:(j + 1) * ff_tile]), 0.0)
        h1_ref[:, j * ff_tile:(j + 1) * ff_tile] = (a * a).astype(BF16)
    y2 = _dot(h1_ref[...], w2_ref[...])
    out_ref[...] = x1 + mod[5:6] * (_rms(y2) * g[3:4])


def _post(o, x, mod, g4, wo, w1, w2, seq):
    t, d = x.shape
    tm = TOKEN_TILE
    per_seq = seq // tm
    d_ff = w1.shape[1]
    row = lambda i: (i, 0)
    return pl.pallas_call(
        functools.partial(_post_kernel, ff_tile=1024),
        out_shape=jax.ShapeDtypeStruct((t, d), F32),
        grid=(t // tm,),
        in_specs=[pl.BlockSpec((tm, o.shape[1]), row),
                  pl.BlockSpec((tm, d), row),
                  pl.BlockSpec((1, 6, d), lambda i: (i // per_seq, 0, 0)),
                  _resident((4, d)),
                  _resident(wo.shape), _resident(w1.shape), _resident(w2.shape)],
        out_specs=pl.BlockSpec((tm, d), row),
        scratch_shapes=[pltpu.VMEM((tm, d_ff), BF16)],
        input_output_aliases={1: 0},
        compiler_params=_params("arbitrary"),
        name="post_ffn",
    )(o, x, mod, g4, wo, w1, w2)


def _gdn_pre_kernel(x_ref, mod_ref, g_ref, win_ref, conv_ref, gate_ref,
                    q_ref, k_ref, v_ref, z_ref, gates_ref, buf_ref, *, per_seq):
    tm = x_ref.shape[0]
    d_qkv = 3 * HEADS * HEAD_DIM
    d_head = HEADS * HEAD_DIM
    halo = 8
    mod = mod_ref[0]
    h = _rms(x_ref[...]) * g_ref[...] * (1.0 + mod[1:2]) + mod[0:1]
    proj = _dot(h.astype(BF16), win_ref[...])

    @pl.when(pl.program_id(0) % per_seq == 0)
    def _():
        buf_ref[0:halo, :] = jnp.zeros((halo, d_qkv), F32)

    buf_ref[halo:halo + tm, :] = proj[:, :d_qkv]
    conv = conv_ref[0:1, :] * buf_ref[halo - 3:halo - 3 + tm, :]
    for tap in range(1, CONV_TAPS):
        conv = conv + conv_ref[tap:tap + 1, :] * buf_ref[halo - 3 + tap:halo - 3 + tap + tm, :]
    buf_ref[0:halo, :] = buf_ref[tm:tm + halo, :]
    act = conv * jax.nn.sigmoid(conv)

    for hd in range(HEADS):
        lo = hd * HEAD_DIM
        qh = act[:, lo:lo + HEAD_DIM]
        kh = act[:, d_head + lo:d_head + lo + HEAD_DIM]
        qn = qh * lax.rsqrt(jnp.sum(qh * qh, axis=-1, keepdims=True) + 1e-6)
        q_ref[:, lo:lo + HEAD_DIM] = (qn * (HEAD_DIM ** -0.5)).astype(BF16)
        k_ref[:, lo:lo + HEAD_DIM] = (
            kh * lax.rsqrt(jnp.sum(kh * kh, axis=-1, keepdims=True) + 1e-6)).astype(BF16)
    v_ref[...] = act[:, 2 * d_head:].astype(BF16)
    z = proj[:, d_qkv:d_qkv + d_head]
    z_ref[...] = (z * jax.nn.sigmoid(z)).astype(BF16)

    raw = proj[:, d_qkv + d_head:]
    gp = gate_ref[...]
    beta = jax.nn.sigmoid(raw)
    gl = gp[0:1] * jax.nn.softplus(raw + gp[1:2])
    pos = lax.broadcasted_iota(jnp.int32, (tm, LANES), 0) % CHUNK
    shift = 1
    while shift < CHUNK:
        gl = gl + jnp.where(pos >= shift, pltpu.roll(gl, shift, 0), 0.0)
        shift *= 2
    lane = lax.broadcasted_iota(jnp.int32, (tm, LANES), 1)
    gates_ref[...] = jnp.where(lane < HEADS, beta, gl)


def _gdn_weights(w_in, a_log, dt_bias):
    d_head = HEADS * HEAD_DIM
    d_qkv = 3 * d_head
    pad = jnp.zeros((w_in.shape[0], LANES - 2 * HEADS), F32)
    win = jnp.concatenate([w_in[:, :d_qkv], w_in[:, d_qkv + 2 * HEADS:],
                           w_in[:, d_qkv:d_qkv + 2 * HEADS], pad], axis=1).astype(BF16)
    zeros = jnp.zeros((HEADS,), F32)
    lane_pad = jnp.zeros((LANES - 2 * HEADS,), F32)
    neg_a = jnp.concatenate([zeros, -jnp.exp(a_log.astype(F32)), lane_pad])
    dtb = jnp.concatenate([zeros, dt_bias.astype(F32), lane_pad])
    return win, jnp.stack([neg_a, dtb])


def _gdn_pre(x, mod, g, win, conv_w, gate_p, seq):
    t, d = x.shape
    tm = TOKEN_TILE
    per_seq = seq // tm
    d_head = HEADS * HEAD_DIM
    row = lambda i: (i, 0)
    tok = jax.ShapeDtypeStruct((t, d_head), BF16)
    return pl.pallas_call(
        functools.partial(_gdn_pre_kernel, per_seq=per_seq),
        out_shape=(tok, tok, tok, tok, jax.ShapeDtypeStruct((t, LANES), F32)),
        grid=(t // tm,),
        in_specs=[pl.BlockSpec((tm, d), row),
                  pl.BlockSpec((1, 6, d), lambda i: (i // per_seq, 0, 0)),
                  _resident((1, d)), _resident(win.shape), _resident(conv_w.shape),
                  _resident((2, LANES))],
        out_specs=(pl.BlockSpec((tm, d_head), row),) * 4 + (pl.BlockSpec((tm, LANES), row),),
        scratch_shapes=[pltpu.VMEM((tm + 8, 3 * d_head), F32)],
        compiler_params=_params("arbitrary"),
        name="gdn_pre",
    )(x, mod, g.reshape(1, d), win, conv_w, gate_p)


def _split(a):
    hi = a.astype(BF16)
    return hi, (a - hi.astype(F32)).astype(BF16)


def _dot_x3(a_hi, a_lo, b_hi, b_lo):
    return _dot(jnp.concatenate([a_hi, a_lo, a_hi], axis=1), jnp.concatenate([b_hi, b_hi, b_lo], axis=0))


def _gdn_core_kernel(q_ref, k_ref, v_ref, z_ref, gates_ref, gt_ref, gout_ref, o_ref, state_ref):
    seg = q_ref.shape[0]
    n_chunks = seg // CHUNK
    heads = range(HEADS)

    @pl.when(pl.program_id(1) == 0)
    def _():
        state_ref[...] = jnp.zeros(state_ref.shape, F32)

    row = lax.broadcasted_iota(jnp.int32, (seg, seg), 0)
    col = lax.broadcasted_iota(jnp.int32, (seg, seg), 1)
    same = (row // CHUNK) == (col // CHUNK)
    lower = jnp.where(same & (row >= col), 1.0, 0.0)
    strict = jnp.where(same & (row > col), 1.0, 0.0)
    eye = jnp.where(row == col, 1.0, 0.0)
    gout = gout_ref[...]

    cols = [slice(hd * HEAD_DIM, (hd + 1) * HEAD_DIM) for hd in heads]
    kf, g_col, g_row, low, a_qk, rhs, q_dec = [], [], [], [], [], [], []
    for hd in heads:
        k_b = k_ref[:, cols[hd]]
        q_b = q_ref[:, cols[hd]]
        kf.append(k_b.astype(F32))
        beta = gates_ref[:, hd:hd + 1]
        g_col.append(gates_ref[:, HEADS + hd:HEADS + hd + 1])
        g_row.append(gt_ref[HEADS + hd:HEADS + hd + 1, :])
        decay = jnp.exp(jnp.minimum(g_col[hd] - g_row[hd], 0.0))
        kb = kf[hd] * beta
        low.append(_dot_nt(kb.astype(BF16), k_b) * (decay * strict))
        a_qk.append(_dot_nt(q_b, k_b) * (decay * lower))
        e_col = jnp.exp(g_col[hd])
        rhs.append(jnp.concatenate([v_ref[:, cols[hd]].astype(F32) * beta, kb * e_col], axis=1))
        q_dec.append(q_b.astype(F32) * e_col)

    inv = [eye - low[hd] for hd in heads]
    power = [_split(low[hd]) for hd in heads]
    order = 2
    while order < CHUNK:
        power = [_split(_dot_x3(*power[hd], *power[hd])) for hd in heads]
        inv = [inv[hd] + _dot_x3(*_split(inv[hd]), *power[hd]) for hd in heads]
        order *= 2
    uw = [_dot_x3(*_split(inv[hd]), *_split(rhs[hd])) for hd in heads]

    state = [state_ref[hd] for hd in heads]
    for n in range(n_chunks):
        rows = slice(n * CHUNK, (n + 1) * CHUNK)
        for hd in heads:
            g_last = g_row[hd][:, (n + 1) * CHUNK - 1:(n + 1) * CHUNK]
            lhs = jnp.concatenate([uw[hd][rows, HEAD_DIM:], q_dec[hd][rows]], axis=0).astype(BF16)
            ws = _dot(lhs, state[hd].astype(BF16))
            v_new = (uw[hd][rows, :HEAD_DIM] - ws[:CHUNK]).astype(BF16)
            o = ws[CHUNK:] + _dot(a_qk[hd][rows, rows].astype(BF16), v_new)
            k_dec = (kf[hd][rows] * jnp.exp(g_last - g_col[hd][rows])).astype(BF16)
            state[hd] = state[hd] * jnp.exp(g_last) + _dot_tn(k_dec, v_new)
            o = _rms(o) * gout * z_ref[rows, cols[hd]].astype(F32)
            o_ref[rows, cols[hd]] = o.astype(o_ref.dtype)
    for hd in heads:
        state_ref[hd] = state[hd]


def _gdn_core(q, k, v, z, gates, gates_t, gout, batch, seq):
    t, d_head = q.shape
    seg = GDN_SEG
    per_seq = seq // seg
    blk = pl.BlockSpec((seg, d_head), lambda b, s: (b * per_seq + s, 0))
    return pl.pallas_call(
        _gdn_core_kernel,
        out_shape=jax.ShapeDtypeStruct((t, d_head), BF16),
        grid=(batch, per_seq),
        in_specs=[blk, blk, blk, blk,
                  pl.BlockSpec((seg, LANES), lambda b, s: (b * per_seq + s, 0)),
                  pl.BlockSpec((2 * HEADS, seg), lambda b, s: (0, b * per_seq + s)),
                  pl.BlockSpec((1, HEAD_DIM), lambda b, s: (0, 0))],
        out_specs=blk,
        scratch_shapes=[pltpu.VMEM((HEADS, HEAD_DIM, HEAD_DIM), F32)],
        compiler_params=_params("arbitrary", "arbitrary"),
        name="gdn_core",
    )(q, k, v, z, gates, gates_t, gout.reshape(1, HEAD_DIM))


def kernel(x, c, positions, ada_w, ada_b, sandwich_g, mla_w_in, mla_q_norm_g, mla_kv_norm_g, mla_w_uq, mla_w_ukv, mla_w_o, gdn_w_in, gdn_conv_w, gdn_a_log, gdn_dt_bias, gdn_out_norm_g, gdn_w_o, ffn_w1, ffn_w2):
    batch, seq, d = x.shape
    depth = ada_w.shape[0]
    t = batch * seq
    assert seq % TOKEN_TILE == 0 and seq % ATTN_TILE == 0 and seq % GDN_SEG == 0
    mod = _adaln(c, ada_w, ada_b).reshape(depth, batch, 6, d)
    cos, sin = _rope_tables(positions)
    xs = x.reshape(t, d)
    for i in range(depth):
        j = i // 2
        if i % 2 == 0:
            dq, dkv = mla_q_norm_g.shape[-1], mla_kv_norm_g.shape[-1]
            win, wuq, wukv = _mla_weights(mla_w_in[j], mla_w_uq[j], mla_w_ukv[j], dq, dkv)
            q, k, v = _mla_pre(xs, mod[i], sandwich_g[i, 0], cos, sin, win, mla_q_norm_g[j],
                               mla_kv_norm_g[j], wuq, wukv, seq)
            o = _attention(q, k, v, batch, seq)
            wo = mla_w_o[j]
        else:
            win, gate_p = _gdn_weights(gdn_w_in[j], gdn_a_log[j], gdn_dt_bias[j])
            q, k, v, z, gates = _gdn_pre(xs, mod[i], sandwich_g[i, 0], win, gdn_conv_w[j], gate_p, seq)
            gates_t = gates[:, :2 * HEADS].T
            o = _gdn_core(q, k, v, z, gates, gates_t, gdn_out_norm_g[j], batch, seq)
            wo = gdn_w_o[j]
        xs = _post(o, xs, mod[i], sandwich_g[i], wo.astype(BF16), ffn_w1[i].astype(BF16),
                   ffn_w2[i].astype(BF16), seq)
    return xs.reshape(batch, seq, d)
```

```python
import functools
import math

import jax
import jax.numpy as jnp
from jax import lax
from jax.experimental import pallas as pl
from jax.experimental.pallas import tpu as pltpu

F32 = jnp.float32
BF16 = jnp.bfloat16

LANES = 128
CHUNK = 64
HEADS = 8
HEAD_DIM = 128
ROPE_DIM = 64
NORM_EPS = 1e-6
ROPE_THETA = 10000.0
CONV_TAPS = 4
VMEM_LIMIT = 56 * 1024 * 1024

TOKEN_TILE = 512
ATTN_TILE = 1024
ATTN_ROWS = 256
GDN_SEG = 256
PRE_ROWS = 256


def _rms(x):
    return x * lax.rsqrt(jnp.mean(x * x, axis=-1, keepdims=True) + NORM_EPS)


def _dot(a, b):
    return jnp.dot(a, b, preferred_element_type=F32)


def _dot_nt(a, b):
    return lax.dot_general(a, b, (((1,), (1,)), ((), ())), preferred_element_type=F32)


def _dot_tn(a, b):
    return lax.dot_general(a, b, (((0,), (0,)), ((), ())), preferred_element_type=F32)


def _params(*sem):
    return pltpu.CompilerParams(dimension_semantics=sem, vmem_limit_bytes=VMEM_LIMIT)


def _resident(shape):
    return pl.BlockSpec(shape, lambda *_: (0,) * len(shape), pipeline_mode=pl.Buffered(1))


def _adaln_kernel(c_ref, w_ref, b_ref, o_ref):
    c = c_ref[...]
    act = (c * jax.nn.sigmoid(c)).astype(BF16)
    o_ref[0] = _dot(act, w_ref[0].astype(BF16)) + b_ref[0]


def _adaln(c, ada_w, ada_b):
    depth, d, n = ada_w.shape
    b = c.shape[0]
    tn = n // 4
    return pl.pallas_call(
        _adaln_kernel,
        out_shape=jax.ShapeDtypeStruct((depth, b, n), F32),
        grid=(depth, n // tn),
        in_specs=[pl.BlockSpec((b, d), lambda i, j: (0, 0)),
                  pl.BlockSpec((1, d, tn), lambda i, j: (i, 0, j)),
                  pl.BlockSpec((1, 1, tn), lambda i, j: (i, 0, j))],
        out_specs=pl.BlockSpec((1, b, tn), lambda i, j: (i, 0, j)),
        compiler_params=_params("arbitrary", "arbitrary"),
        name="adaln_mod",
    )(c, ada_w, ada_b.reshape(depth, 1, n))


def _rope_kernel(pos_ref, inv_ref, cos_ref, sin_ref):
    ang = pos_ref[...].astype(F32) * inv_ref[...]
    cos_ref[...] = jnp.cos(ang)
    sin_ref[...] = jnp.sin(ang)


def _rope_tables(positions):
    t = positions.size
    half = ROPE_DIM // 2
    inv_freq = ROPE_THETA ** (-jnp.arange(half, dtype=F32) / half)
    inv_row = jnp.tile(inv_freq, LANES // half).reshape(1, LANES)
    tm = 1024
    return pl.pallas_call(
        _rope_kernel,
        out_shape=(jax.ShapeDtypeStruct((t, LANES), F32),) * 2,
        grid=(t // tm,),
        in_specs=[pl.BlockSpec((tm, 1), lambda i: (i, 0)),
                  pl.BlockSpec((1, LANES), lambda i: (0, 0))],
        out_specs=(pl.BlockSpec((tm, LANES), lambda i: (i, 0)),) * 2,
        compiler_params=_params("arbitrary"),
        name="rope_tables",
    )(positions.reshape(t, 1), inv_row)


def _mla_pre_kernel(x_ref, mod_ref, g_ref, cos_ref, sin_ref, win_ref, gq_ref, gkv_ref,
                    wuq_ref, wukv_ref, q_ref, k_ref, v_ref, *, dq, dkv, scale):
    d_nope = HEADS * HEAD_DIM
    mod = mod_ref[0]
    h = _rms(x_ref[...]) * g_ref[...] * (1.0 + mod[1:2]) + mod[0:1]
    proj = _dot(h.astype(BF16), win_ref[...])
    cos = cos_ref[...]
    sin = sin_ref[...]
    k_rope = (proj[:, dq + dkv:dq + dkv + LANES] * cos
              + proj[:, dq + dkv + LANES:dq + dkv + 2 * LANES] * sin).astype(BF16)
    qn = (_rms(proj[:, :dq]) * gq_ref[...]).astype(BF16)
    qf = _dot(qn, wuq_ref[...])
    cos_h = jnp.concatenate([cos] * HEADS, axis=1)
    sin_h = jnp.concatenate([sin] * HEADS, axis=1)
    q_nope = (qf[:, :d_nope] * scale).astype(BF16)
    q_rope = ((qf[:, d_nope:2 * d_nope] * cos_h + qf[:, 2 * d_nope:] * sin_h) * scale).astype(BF16)
    kvn = (_rms(proj[:, dq:dq + dkv]) * gkv_ref[...]).astype(BF16)
    kv = _dot(kvn, wukv_ref[...])
    k_nope = kv[:, :d_nope].astype(BF16)
    v = kv[:, d_nope:].astype(BF16)
    ones = jnp.ones((v.shape[0], HEAD_DIM), BF16)
    for hd in range(HEADS):
        lo = hd * HEAD_DIM
        q_ref[:, 2 * lo:2 * lo + HEAD_DIM] = q_nope[:, lo:lo + HEAD_DIM]
        q_ref[:, 2 * lo + HEAD_DIM:2 * lo + 2 * HEAD_DIM] = q_rope[:, lo:lo + HEAD_DIM]
        k_ref[:, 2 * lo:2 * lo + HEAD_DIM] = k_nope[:, lo:lo + HEAD_DIM]
        k_ref[:, 2 * lo + HEAD_DIM:2 * lo + 2 * HEAD_DIM] = k_rope
        v_ref[:, 2 * lo:2 * lo + HEAD_DIM] = v[:, lo:lo + HEAD_DIM]
        v_ref[:, 2 * lo + HEAD_DIM:2 * lo + 2 * HEAD_DIM] = ones


def _swap_rot(w):
    half = ROPE_DIM // 2
    return jnp.concatenate([-w[..., half:], w[..., :half]], axis=-1)


def _mla_weights(w_in, w_uq, w_ukv, dq, dkv):
    d = w_in.shape[0]
    wkr = w_in[:, dq + dkv:]
    wkr_sw = _swap_rot(wkr)
    win = jnp.concatenate([w_in[:, :dq + dkv], wkr, wkr, wkr_sw, wkr_sw], axis=1).astype(BF16)
    wq = w_uq.reshape(dq, HEADS, HEAD_DIM + ROPE_DIM)
    nope = wq[:, :, :HEAD_DIM].reshape(dq, HEADS * HEAD_DIM)
    rope = wq[:, :, HEAD_DIM:]
    pad = jnp.zeros((dq, HEADS, HEAD_DIM - ROPE_DIM), F32)
    rope_p = jnp.concatenate([rope, pad], axis=-1).reshape(dq, HEADS * HEAD_DIM)
    rope_sw = jnp.concatenate([_swap_rot(rope), pad], axis=-1).reshape(dq, HEADS * HEAD_DIM)
    wuq = jnp.concatenate([nope, rope_p, rope_sw], axis=1).astype(BF16)
    wkv = w_ukv.reshape(dkv, HEADS, 2, HEAD_DIM).transpose(0, 2, 1, 3).reshape(dkv, 2 * HEADS * HEAD_DIM)
    del d
    return win, wuq, wkv.astype(BF16)


def _mla_pre(x, mod, g, cos, sin, win, gq, gkv, wuq, wukv, seq):
    t, d = x.shape
    dq, dkv = gq.shape[-1], gkv.shape[-1]
    tm = TOKEN_TILE
    per_seq = seq // tm
    d_nope = HEADS * HEAD_DIM
    scale = float((HEAD_DIM + ROPE_DIM) ** -0.5 * math.log2(math.e))
    row = lambda i: (i, 0)
    return pl.pallas_call(
        functools.partial(_mla_pre_kernel, dq=dq, dkv=dkv, scale=scale),
        out_shape=(jax.ShapeDtypeStruct((t, 2 * d_nope), BF16),
                   jax.ShapeDtypeStruct((t, 2 * d_nope), BF16),
                   jax.ShapeDtypeStruct((t, 2 * d_nope), BF16)),
        grid=(t // tm,),
        in_specs=[pl.BlockSpec((tm, d), row),
                  pl.BlockSpec((1, 6, d), lambda i: (i // per_seq, 0, 0)),
                  _resident((1, d)),
                  pl.BlockSpec((tm, LANES), row),
                  pl.BlockSpec((tm, LANES), row),
                  _resident(win.shape), _resident((1, dq)), _resident((1, dkv)),
                  _resident(wuq.shape), _resident(wukv.shape)],
        out_specs=(pl.BlockSpec((tm, 2 * d_nope), row),
                   pl.BlockSpec((tm, 2 * d_nope), row),
                   pl.BlockSpec((tm, 2 * d_nope), row)),
        compiler_params=_params("arbitrary"),
        name="mla_pre",
    )(x, mod, g.reshape(1, d), cos, sin, win, gq.reshape(1, dq), gkv.reshape(1, dkv), wuq, wukv)


def _attn_kernel(q_ref, k_ref, v_ref, o_ref, m_ref, acc_ref, bias_ref, *, tile):
    i = pl.program_id(2)

    @pl.when((pl.program_id(0) == 0) & (pl.program_id(1) == 0) & (i == 0))
    def _():
        qc = lax.broadcasted_iota(jnp.int32, (tile, tile), 0) // CHUNK
        kc = lax.broadcasted_iota(jnp.int32, (tile, tile), 1) // CHUNK
        bias_ref[...] = jnp.where(kc <= qc, 0.0, -jnp.inf)

    def step(j, masked, width=tile):
        start = pl.multiple_of(j * width, width)
        k = k_ref[pl.ds(start, width), :]
        v = v_ref[pl.ds(start, width), :]
        blocks = [slice(r * ATTN_ROWS, (r + 1) * ATTN_ROWS) for r in range(tile // ATTN_ROWS)]
        seen = [rows.stop if masked else width for rows in blocks]
        scores = [_dot_nt(q_ref[rows, :], k[:n]) for rows, n in zip(blocks, seen)]
        for rows, n, s in zip(blocks, seen, scores):
            if masked:
                s = s + bias_ref[rows, :n]
                m_new = jnp.max(s, axis=-1, keepdims=True)
                acc_ref[rows, :] = _dot(jnp.exp2(s - m_new).astype(BF16), v[:n])
            else:
                m_prev = m_ref[rows, :]
                m_new = jnp.maximum(m_prev, jnp.max(s, axis=-1, keepdims=True))
                p = jnp.exp2(s - m_new).astype(BF16)
                acc_ref[rows, :] = jnp.exp2(m_prev - m_new) * acc_ref[rows, :] + _dot(p, v[:n])
            m_ref[rows, :] = m_new

    def body(j, carry):
        step(j, False, 2 * tile)
        return carry

    step(i, True)
    lax.fori_loop(0, i // 2, body, 0)

    @pl.when(i % 2 == 1)
    def _():
        step(i - 1, False)
    acc = acc_ref[...]
    o_ref[...] = (acc[:, :HEAD_DIM] / acc[:, HEAD_DIM:]).astype(o_ref.dtype)


def _attention(q, k, v, batch, seq):
    t = q.shape[0]
    tile = ATTN_TILE
    nq = seq // tile
    return pl.pallas_call(
        functools.partial(_attn_kernel, tile=tile),
        out_shape=jax.ShapeDtypeStruct((t, HEADS * HEAD_DIM), BF16),
        grid=(batch, HEADS, nq),
        in_specs=[pl.BlockSpec((tile, 2 * HEAD_DIM), lambda b, h, i: (b * nq + i, h)),
                  pl.BlockSpec((seq, 2 * HEAD_DIM), lambda b, h, i: (b, h)),
                  pl.BlockSpec((seq, 2 * HEAD_DIM), lambda b, h, i: (b, h))],
        out_specs=pl.BlockSpec((tile, HEAD_DIM), lambda b, h, i: (b * nq + i, h)),
        scratch_shapes=[pltpu.VMEM((tile, 1), F32), pltpu.VMEM((tile, 2 * HEAD_DIM), F32),
                        pltpu.VMEM((tile, tile), F32)],
        compiler_params=_params("arbitrary", "arbitrary", "arbitrary"),
        name="mla_attention",
    )(q, k, v)


def _post_kernel(o_ref, x_ref, mod_ref, g_ref, wo_ref, w1_ref, w2_ref, out_ref, h1_ref, *, ff_tile):
    mod = mod_ref[0]
    g = g_ref[...]
    y = _dot(o_ref[...], wo_ref[...])
    x1 = x_ref[...] + mod[2:3] * (_rms(y) * g[1:2])
    h = (_rms(x1) * g[2:3] * (1.0 + mod[4:5]) + mod[3:4]).astype(BF16)
    d_ff = w1_ref.shape[1]
    for j in range(d_ff // ff_tile):
        a = jnp.maximum(_dot(h, w1_ref[:, j * ff_tile:(j + 1) * ff_tile]), 0.0)
        h1_ref[:, j * ff_tile:(j + 1) * ff_tile] = (a * a).astype(BF16)
    y2 = _dot(h1_ref[...], w2_ref[...])
    out_ref[...] = x1 + mod[5:6] * (_rms(y2) * g[3:4])


def _post(o, x, mod, g4, wo, w1, w2, seq):
    t, d = x.shape
    tm = TOKEN_TILE
    per_seq = seq // tm
    d_ff = w1.shape[1]
    row = lambda i: (i, 0)
    return pl.pallas_call(
        functools.partial(_post_kernel, ff_tile=1024),
        out_shape=jax.ShapeDtypeStruct((t, d), F32),
        grid=(t // tm,),
        in_specs=[pl.BlockSpec((tm, o.shape[1]), row),
                  pl.BlockSpec((tm, d), row),
                  pl.BlockSpec((1, 6, d), lambda i: (i // per_seq, 0, 0)),
                  _resident((4, d)),
                  _resident(wo.shape), _resident(w1.shape), _resident(w2.shape)],
        out_specs=pl.BlockSpec((tm, d), row),
        scratch_shapes=[pltpu.VMEM((tm, d_ff), BF16)],
        compiler_params=_params("arbitrary"),
        name="post_ffn",
    )(o, x, mod, g4, wo, w1, w2)


def _gdn_pre_kernel(x_ref, mod_ref, g_ref, win_ref, conv_ref, gate_ref,
                    q_ref, k_ref, v_ref, z_ref, gates_ref, buf_ref, *, per_seq):
    tm = x_ref.shape[0]
    d_qkv = 3 * HEADS * HEAD_DIM
    d_head = HEADS * HEAD_DIM
    halo = 8
    mod = mod_ref[0]
    gp = gate_ref[...]

    @pl.when(pl.program_id(0) % per_seq == 0)
    def _():
        buf_ref[0:halo, :] = jnp.zeros((halo, d_qkv), F32)

    blocks = [slice(r * PRE_ROWS, (r + 1) * PRE_ROWS) for r in range(tm // PRE_ROWS)]
    projs = []
    for rows in blocks:
        h = _rms(x_ref[rows, :]) * g_ref[...] * (1.0 + mod[1:2]) + mod[0:1]
        projs.append(_dot(h.astype(BF16), win_ref[...]))
    for rows, proj in zip(blocks, projs):
        buf_ref[halo + rows.start:halo + rows.stop, :] = proj[:, :d_qkv]
    last_rows = buf_ref[tm:tm + halo, :]

    for rows, proj in zip(blocks, projs):
        base = halo - (CONV_TAPS - 1) + rows.start
        conv = conv_ref[0:1, :] * buf_ref[base:base + PRE_ROWS, :]
        for tap in range(1, CONV_TAPS):
            conv = conv + conv_ref[tap:tap + 1, :] * buf_ref[base + tap:base + tap + PRE_ROWS, :]
        act = conv * jax.nn.sigmoid(conv)

        for hd in range(HEADS):
            lo = hd * HEAD_DIM
            qh = act[:, lo:lo + HEAD_DIM]
            kh = act[:, d_head + lo:d_head + lo + HEAD_DIM]
            qn = qh * lax.rsqrt(jnp.sum(qh * qh, axis=-1, keepdims=True) + 1e-6)
            q_ref[rows, lo:lo + HEAD_DIM] = (qn * (HEAD_DIM ** -0.5)).astype(BF16)
            k_ref[rows, lo:lo + HEAD_DIM] = (
                kh * lax.rsqrt(jnp.sum(kh * kh, axis=-1, keepdims=True) + 1e-6)).astype(BF16)
        v_ref[rows, :] = act[:, 2 * d_head:].astype(BF16)
        z = proj[:, d_qkv:d_qkv + d_head]
        z_ref[rows, :] = (z * jax.nn.sigmoid(z)).astype(BF16)

        raw = proj[:, d_qkv + d_head:]
        beta = jax.nn.sigmoid(raw)
        gl = gp[0:1] * jax.nn.softplus(raw + gp[1:2])
        pos = lax.broadcasted_iota(jnp.int32, (PRE_ROWS, LANES), 0) % CHUNK
        shift = 1
        while shift < CHUNK:
            gl = gl + jnp.where(pos >= shift, pltpu.roll(gl, shift, 0), 0.0)
            shift *= 2
        lane = lax.broadcasted_iota(jnp.int32, (PRE_ROWS, LANES), 1)
        gates_ref[rows, :] = jnp.where(lane < HEADS, beta, gl)
    buf_ref[0:halo, :] = last_rows


def _gdn_weights(w_in, a_log, dt_bias):
    d_head = HEADS * HEAD_DIM
    d_qkv = 3 * d_head
    pad = jnp.zeros((w_in.shape[0], LANES - 2 * HEADS), F32)
    win = jnp.concatenate([w_in[:, :d_qkv], w_in[:, d_qkv + 2 * HEADS:],
                           w_in[:, d_qkv:d_qkv + 2 * HEADS], pad], axis=1).astype(BF16)
    zeros = jnp.zeros((HEADS,), F32)
    lane_pad = jnp.zeros((LANES - 2 * HEADS,), F32)
    neg_a = jnp.concatenate([zeros, -jnp.exp(a_log.astype(F32)), lane_pad])
    dtb = jnp.concatenate([zeros, dt_bias.astype(F32), lane_pad])
    return win, jnp.stack([neg_a, dtb])


def _gdn_pre(x, mod, g, win, conv_w, gate_p, seq):
    t, d = x.shape
    tm = TOKEN_TILE
    per_seq = seq // tm
    d_head = HEADS * HEAD_DIM
    row = lambda i: (i, 0)
    tok = jax.ShapeDtypeStruct((t, d_head), BF16)
    return pl.pallas_call(
        functools.partial(_gdn_pre_kernel, per_seq=per_seq),
        out_shape=(tok, tok, tok, tok, jax.ShapeDtypeStruct((t, LANES), F32)),
        grid=(t // tm,),
        in_specs=[pl.BlockSpec((tm, d), row),
                  pl.BlockSpec((1, 6, d), lambda i: (i // per_seq, 0, 0)),
                  _resident((1, d)), _resident(win.shape), _resident(conv_w.shape),
                  _resident((2, LANES))],
        out_specs=(pl.BlockSpec((tm, d_head), row),) * 4 + (pl.BlockSpec((tm, LANES), row),),
        scratch_shapes=[pltpu.VMEM((tm + 8, 3 * d_head), F32)],
        compiler_params=_params("arbitrary"),
        name="gdn_pre",
    )(x, mod, g.reshape(1, d), win, conv_w, gate_p)


def _gdn_core_kernel(q_ref, k_ref, v_ref, z_ref, gates_ref, gt_ref, gout_ref, o_ref, state_ref):
    seg = q_ref.shape[0]
    n_chunks = seg // CHUNK
    n_tiles = seg // LANES
    heads = range(HEADS)

    @pl.when(pl.program_id(1) == 0)
    def _():
        state_ref[...] = jnp.zeros(state_ref.shape, F32)

    row = lax.broadcasted_iota(jnp.int32, (seg, seg), 0)
    col = lax.broadcasted_iota(jnp.int32, (seg, seg), 1)
    same = (row // CHUNK) == (col // CHUNK)
    lower = jnp.where(same & (row >= col), 1.0, 0.0)
    strict = jnp.where(same & (row > col), 1.0, 0.0)
    blocks = jnp.where(same, 1.0, 0.0)
    gout = gout_ref[...]

    t_row = lax.broadcasted_iota(jnp.int32, (seg, LANES), 0)
    t_lane = lax.broadcasted_iota(jnp.int32, (seg, LANES), 1)
    t_own = (t_lane // CHUNK) == ((t_row // CHUNK) % 2)
    t_i, t_j = t_row % CHUNK, t_lane % CHUNK
    w_i = lax.broadcasted_iota(jnp.int32, (CHUNK, seg), 0)
    w_lane = lax.broadcasted_iota(jnp.int32, (CHUNK, seg), 1)
    w_chunk, w_j = w_lane // CHUNK, w_lane % CHUNK
    lane_chunk = lax.broadcasted_iota(jnp.int32, (1, seg), 1) // CHUNK

    def tall(full):
        t = full[:, :LANES]
        for i in range(1, n_tiles):
            t = jnp.where(t_row // LANES == i, full[:, i * LANES:(i + 1) * LANES], t)
        return jnp.where(t_own, t, 0.0)

    def wide(full):
        w = full[:CHUNK]
        for c in range(1, n_chunks):
            w = jnp.where(w_chunk == c, full[c * CHUNK:(c + 1) * CHUNK], w)
        return w

    def left(t):
        return t.astype(BF16)

    def right(w):
        w = w.astype(BF16)
        return jnp.concatenate([w, w], axis=0)

    cols = [slice(hd * HEAD_DIM, (hd + 1) * HEAD_DIM) for hd in heads]
    g_row, low, a_qk, rhs, q_dec, kd_t = [], [], [], [], [], []
    for hd in heads:
        k_b = k_ref[:, cols[hd]]
        q_b = q_ref[:, cols[hd]]
        kf = k_b.astype(F32)
        beta = gates_ref[:, hd:hd + 1]
        g_col = gates_ref[:, HEADS + hd:HEADS + hd + 1]
        g_row.append(gt_ref[HEADS + hd:HEADS + hd + 1, :])
        decay = jnp.exp(jnp.minimum(g_col - g_row[hd], 0.0))
        kb = kf * beta
        low.append(_dot_nt(kb.astype(BF16), k_b) * (decay * strict))
        a_qk.append((_dot_nt(q_b, k_b) * (decay * lower)).astype(BF16))
        e_col = jnp.exp(g_col)
        rhs.append(jnp.concatenate([v_ref[:, cols[hd]].astype(F32) * beta, kb * e_col], axis=1))
        q_dec.append(q_b.astype(F32) * e_col)
        g_end = g_row[hd][:, CHUNK - 1:CHUNK]
        for c in range(1, n_chunks):
            g_end = jnp.where(lane_chunk == c, g_row[hd][:, (c + 1) * CHUNK - 1:(c + 1) * CHUNK], g_end)
        kd_t.append((kf.T * jnp.exp(g_end - g_row[hd])).astype(BF16))

    low_t = [tall(low[hd]) for hd in heads]
    inv_t = [jnp.where(t_own & (t_i == t_j), 1.0, 0.0) - jnp.where(t_i // 2 == t_j // 2, low_t[hd], 0.0)
             for hd in heads]
    inv_w = [jnp.where(w_i == w_j, 1.0, 0.0) - jnp.where(w_i // 2 == w_j // 2, wide(low[hd]), 0.0)
             for hd in heads]
    size = 2
    while size < CHUNK:
        joins = (t_i // (2 * size) == t_j // (2 * size)) & (t_i // size != t_j // size)
        step = [_dot(left(jnp.where(joins, low_t[hd], 0.0)), right(inv_w[hd])) for hd in heads]
        step = [_dot(left(inv_t[hd]), right(wide(step[hd]))) for hd in heads]
        inv_t = [inv_t[hd] - tall(step[hd]) for hd in heads]
        if 2 * size < CHUNK:
            inv_w = [inv_w[hd] - wide(step[hd]) for hd in heads]
        size *= 2
    inv = [(jnp.concatenate([inv_t[hd]] * n_tiles, axis=1) * blocks).astype(BF16) for hd in heads]
    uw = [_dot(inv[hd], rhs[hd].astype(BF16)) for hd in heads]

    state = [state_ref[hd] for hd in heads]
    zero_rows = jnp.zeros((CHUNK, HEAD_DIM), BF16)
    for n in range(n_chunks):
        rows = slice(n * CHUNK, (n + 1) * CHUNK)
        tile = slice((n // 2) * LANES, (n // 2 + 1) * LANES)
        ws, mix = [], []
        for hd in heads:
            lhs = jnp.concatenate([uw[hd][rows, HEAD_DIM:], q_dec[hd][rows]], axis=0).astype(BF16)
            ws.append(_dot(lhs, state[hd].astype(BF16)))
        for hd in heads:
            v_new = (uw[hd][rows, :HEAD_DIM] - ws[hd][:CHUNK]).astype(BF16)
            v_two = jnp.concatenate([v_new, zero_rows] if n % 2 == 0 else [zero_rows, v_new], axis=0)
            lhs = jnp.concatenate([a_qk[hd][rows, tile], kd_t[hd][:, tile]], axis=0)
            mix.append(_dot(lhs, v_two))
        for hd in heads:
            g_last = g_row[hd][:, (n + 1) * CHUNK - 1:(n + 1) * CHUNK]
            state[hd] = state[hd] * jnp.exp(g_last) + mix[hd][CHUNK:]
            o = _rms(ws[hd][CHUNK:] + mix[hd][:CHUNK]) * gout * z_ref[rows, cols[hd]].astype(F32)
            o_ref[rows, cols[hd]] = o.astype(o_ref.dtype)
    for hd in heads:
        state_ref[hd] = state[hd]


def _gdn_core(q, k, v, z, gates, gates_t, gout, batch, seq):
    t, d_head = q.shape
    seg = GDN_SEG
    per_seq = seq // seg
    blk = pl.BlockSpec((seg, d_head), lambda b, s: (b * per_seq + s, 0))
    return pl.pallas_call(
        _gdn_core_kernel,
        out_shape=jax.ShapeDtypeStruct((t, d_head), BF16),
        grid=(batch, per_seq),
        in_specs=[blk, blk, blk, blk,
                  pl.BlockSpec((seg, LANES), lambda b, s: (b * per_seq + s, 0)),
                  pl.BlockSpec((2 * HEADS, seg), lambda b, s: (0, b * per_seq + s)),
                  pl.BlockSpec((1, HEAD_DIM), lambda b, s: (0, 0))],
        out_specs=blk,
        scratch_shapes=[pltpu.VMEM((HEADS, HEAD_DIM, HEAD_DIM), F32)],
        compiler_params=_params("arbitrary", "arbitrary"),
        name="gdn_core",
    )(q, k, v, z, gates, gates_t, gout.reshape(1, HEAD_DIM))


def kernel(x, c, positions, ada_w, ada_b, sandwich_g, mla_w_in, mla_q_norm_g, mla_kv_norm_g, mla_w_uq, mla_w_ukv, mla_w_o, gdn_w_in, gdn_conv_w, gdn_a_log, gdn_dt_bias, gdn_out_norm_g, gdn_w_o, ffn_w1, ffn_w2):
    batch, seq, d = x.shape
    depth = ada_w.shape[0]
    t = batch * seq
    assert seq % TOKEN_TILE == 0 and seq % ATTN_TILE == 0 and seq % GDN_SEG == 0
    mod = _adaln(c, ada_w, ada_b).reshape(depth, batch, 6, d)
    cos, sin = _rope_tables(positions)
    xs = x.reshape(t, d)
    for i in range(depth):
        j = i // 2
        if i % 2 == 0:
            dq, dkv = mla_q_norm_g.shape[-1], mla_kv_norm_g.shape[-1]
            win, wuq, wukv = _mla_weights(mla_w_in[j], mla_w_uq[j], mla_w_ukv[j], dq, dkv)
            q, k, v = _mla_pre(xs, mod[i], sandwich_g[i, 0], cos, sin, win, mla_q_norm_g[j],
                               mla_kv_norm_g[j], wuq, wukv, seq)
            o = _attention(q, k, v, batch, seq)
            wo = mla_w_o[j]
        else:
            win, gate_p = _gdn_weights(gdn_w_in[j], gdn_a_log[j], gdn_dt_bias[j])
            q, k, v, z, gates = _gdn_pre(xs, mod[i], sandwich_g[i, 0], win, gdn_conv_w[j], gate_p, seq)
            gates_t = gates[:, :2 * HEADS].T
            o = _gdn_core(q, k, v, z, gates, gates_t, gdn_out_norm_g[j], batch, seq)
            wo = gdn_w_o[j]
        xs = _post(o, xs, mod[i], sandwich_g[i], wo.astype(BF16), ffn_w1[i].astype(BF16),
                   ffn_w2[i].astype(BF16), seq)
    return xs.reshape(batch, seq, d)
```

```python
import functools
import math

import jax
import jax.numpy as jnp
from jax import lax
from jax.experimental import pallas as pl
from jax.experimental.pallas import tpu as pltpu

F32 = jnp.float32
BF16 = jnp.bfloat16

LANES = 128
CHUNK = 64
HEADS = 8
HEAD_DIM = 128
ROPE_DIM = 64
NORM_EPS = 1e-6
ROPE_THETA = 10000.0
CONV_TAPS = 4
VMEM_LIMIT = 56 * 1024 * 1024

TOKEN_TILE = 512
ATTN_TILE = 1024
ATTN_ROWS = 256
GDN_SEG = 256
POST_ROWS = 256
PRE_ROWS = 256


def _rms(x):
    return x * lax.rsqrt(jnp.mean(x * x, axis=-1, keepdims=True) + NORM_EPS)


def _dot(a, b):
    return jnp.dot(a, b, preferred_element_type=F32)


def _dot_nt(a, b):
    return lax.dot_general(a, b, (((1,), (1,)), ((), ())), preferred_element_type=F32)


def _dot_tn(a, b):
    return lax.dot_general(a, b, (((0,), (0,)), ((), ())), preferred_element_type=F32)


def _params(*sem):
    return pltpu.CompilerParams(dimension_semantics=sem, vmem_limit_bytes=VMEM_LIMIT)


def _resident(shape):
    return pl.BlockSpec(shape, lambda *_: (0,) * len(shape), pipeline_mode=pl.Buffered(1))


def _adaln_kernel(c_ref, w_ref, b_ref, o_ref):
    c = c_ref[...]
    act = (c * jax.nn.sigmoid(c)).astype(BF16)
    o_ref[0] = _dot(act, w_ref[0].astype(BF16)) + b_ref[0]


def _adaln(c, ada_w, ada_b):
    depth, d, n = ada_w.shape
    b = c.shape[0]
    tn = n // 4
    return pl.pallas_call(
        _adaln_kernel,
        out_shape=jax.ShapeDtypeStruct((depth, b, n), F32),
        grid=(depth, n // tn),
        in_specs=[pl.BlockSpec((b, d), lambda i, j: (0, 0)),
                  pl.BlockSpec((1, d, tn), lambda i, j: (i, 0, j)),
                  pl.BlockSpec((1, 1, tn), lambda i, j: (i, 0, j))],
        out_specs=pl.BlockSpec((1, b, tn), lambda i, j: (i, 0, j)),
        compiler_params=_params("arbitrary", "arbitrary"),
        name="adaln_mod",
    )(c, ada_w, ada_b.reshape(depth, 1, n))


def _rope_kernel(pos_ref, inv_ref, cos_ref, sin_ref):
    ang = pos_ref[...].astype(F32) * inv_ref[...]
    cos_ref[...] = jnp.cos(ang)
    sin_ref[...] = jnp.sin(ang)


def _rope_tables(positions):
    t = positions.size
    half = ROPE_DIM // 2
    inv_freq = ROPE_THETA ** (-jnp.arange(half, dtype=F32) / half)
    inv_row = jnp.tile(inv_freq, LANES // half).reshape(1, LANES)
    tm = 1024
    return pl.pallas_call(
        _rope_kernel,
        out_shape=(jax.ShapeDtypeStruct((t, LANES), F32),) * 2,
        grid=(t // tm,),
        in_specs=[pl.BlockSpec((tm, 1), lambda i: (i, 0)),
                  pl.BlockSpec((1, LANES), lambda i: (0, 0))],
        out_specs=(pl.BlockSpec((tm, LANES), lambda i: (i, 0)),) * 2,
        compiler_params=_params("arbitrary"),
        name="rope_tables",
    )(positions.reshape(t, 1), inv_row)


def _mla_pre_kernel(x_ref, mod_ref, g_ref, cos_ref, sin_ref, win_ref, gq_ref, gkv_ref,
                    wuq_ref, wukv_ref, q_ref, k_ref, v_ref, *, dq, dkv, scale):
    d_nope = HEADS * HEAD_DIM
    mod = mod_ref[0]
    h = _rms(x_ref[...]) * g_ref[...] * (1.0 + mod[1:2]) + mod[0:1]
    proj = _dot(h.astype(BF16), win_ref[...])
    cos = cos_ref[...]
    sin = sin_ref[...]
    k_rope = (proj[:, dq + dkv:dq + dkv + LANES] * cos
              + proj[:, dq + dkv + LANES:dq + dkv + 2 * LANES] * sin).astype(BF16)
    qn = (_rms(proj[:, :dq]) * gq_ref[...]).astype(BF16)
    qf = _dot(qn, wuq_ref[...])
    cos_h = jnp.concatenate([cos] * HEADS, axis=1)
    sin_h = jnp.concatenate([sin] * HEADS, axis=1)
    q_nope = (qf[:, :d_nope] * scale).astype(BF16)
    q_rope = ((qf[:, d_nope:2 * d_nope] * cos_h + qf[:, 2 * d_nope:] * sin_h) * scale).astype(BF16)
    kvn = (_rms(proj[:, dq:dq + dkv]) * gkv_ref[...]).astype(BF16)
    kv = _dot(kvn, wukv_ref[...])
    k_nope = kv[:, :d_nope].astype(BF16)
    v = kv[:, d_nope:].astype(BF16)
    ones = jnp.ones((v.shape[0], HEAD_DIM), BF16)
    for hd in range(HEADS):
        lo = hd * HEAD_DIM
        q_ref[:, 2 * lo:2 * lo + HEAD_DIM] = q_nope[:, lo:lo + HEAD_DIM]
        q_ref[:, 2 * lo + HEAD_DIM:2 * lo + 2 * HEAD_DIM] = q_rope[:, lo:lo + HEAD_DIM]
        k_ref[:, 2 * lo:2 * lo + HEAD_DIM] = k_nope[:, lo:lo + HEAD_DIM]
        k_ref[:, 2 * lo + HEAD_DIM:2 * lo + 2 * HEAD_DIM] = k_rope
        v_ref[:, 2 * lo:2 * lo + HEAD_DIM] = v[:, lo:lo + HEAD_DIM]
        v_ref[:, 2 * lo + HEAD_DIM:2 * lo + 2 * HEAD_DIM] = ones


def _swap_rot(w):
    half = ROPE_DIM // 2
    return jnp.concatenate([-w[..., half:], w[..., :half]], axis=-1)


def _mla_weights(w_in, w_uq, w_ukv, dq, dkv):
    d = w_in.shape[0]
    wkr = w_in[:, dq + dkv:]
    wkr_sw = _swap_rot(wkr)
    win = jnp.concatenate([w_in[:, :dq + dkv], wkr, wkr, wkr_sw, wkr_sw], axis=1).astype(BF16)
    wq = w_uq.reshape(dq, HEADS, HEAD_DIM + ROPE_DIM)
    nope = wq[:, :, :HEAD_DIM].reshape(dq, HEADS * HEAD_DIM)
    rope = wq[:, :, HEAD_DIM:]
    pad = jnp.zeros((dq, HEADS, HEAD_DIM - ROPE_DIM), F32)
    rope_p = jnp.concatenate([rope, pad], axis=-1).reshape(dq, HEADS * HEAD_DIM)
    rope_sw = jnp.concatenate([_swap_rot(rope), pad], axis=-1).reshape(dq, HEADS * HEAD_DIM)
    wuq = jnp.concatenate([nope, rope_p, rope_sw], axis=1).astype(BF16)
    wkv = w_ukv.reshape(dkv, HEADS, 2, HEAD_DIM).transpose(0, 2, 1, 3).reshape(dkv, 2 * HEADS * HEAD_DIM)
    del d
    return win, wuq, wkv.astype(BF16)


def _mla_pre(x, mod, g, cos, sin, win, gq, gkv, wuq, wukv, seq):
    t, d = x.shape
    dq, dkv = gq.shape[-1], gkv.shape[-1]
    tm = TOKEN_TILE
    per_seq = seq // tm
    d_nope = HEADS * HEAD_DIM
    scale = float((HEAD_DIM + ROPE_DIM) ** -0.5 * math.log2(math.e))
    row = lambda i: (i, 0)
    return pl.pallas_call(
        functools.partial(_mla_pre_kernel, dq=dq, dkv=dkv, scale=scale),
        out_shape=(jax.ShapeDtypeStruct((t, 2 * d_nope), BF16),
                   jax.ShapeDtypeStruct((t, 2 * d_nope), BF16),
                   jax.ShapeDtypeStruct((t, 2 * d_nope), BF16)),
        grid=(t // tm,),
        in_specs=[pl.BlockSpec((tm, d), row),
                  pl.BlockSpec((1, 6, d), lambda i: (i // per_seq, 0, 0)),
                  _resident((1, d)),
                  pl.BlockSpec((tm, LANES), row),
                  pl.BlockSpec((tm, LANES), row),
                  _resident(win.shape), _resident((1, dq)), _resident((1, dkv)),
                  _resident(wuq.shape), _resident(wukv.shape)],
        out_specs=(pl.BlockSpec((tm, 2 * d_nope), row),
                   pl.BlockSpec((tm, 2 * d_nope), row),
                   pl.BlockSpec((tm, 2 * d_nope), row)),
        compiler_params=_params("arbitrary"),
        name="mla_pre",
    )(x, mod, g.reshape(1, d), cos, sin, win, gq.reshape(1, dq), gkv.reshape(1, dkv), wuq, wukv)


def _attn_kernel(q_ref, k_ref, v_ref, o_ref, m_ref, acc_ref, bias_ref, *, tile):
    i = pl.program_id(2)

    @pl.when((pl.program_id(0) == 0) & (pl.program_id(1) == 0) & (i == 0))
    def _():
        qc = lax.broadcasted_iota(jnp.int32, (tile, tile), 0) // CHUNK
        kc = lax.broadcasted_iota(jnp.int32, (tile, tile), 1) // CHUNK
        bias_ref[...] = jnp.where(kc <= qc, 0.0, -jnp.inf)

    def step(start, width, diag_at=None):
        start = pl.multiple_of(start, tile)
        k = k_ref[pl.ds(start, width), :]
        v = v_ref[pl.ds(start, width), :]
        blocks = [slice(r * ATTN_ROWS, (r + 1) * ATTN_ROWS) for r in range(tile // ATTN_ROWS)]
        seen = [width if diag_at is None else diag_at + rows.stop for rows in blocks]
        scores = [_dot_nt(q_ref[rows, :], k[:n]) for rows, n in zip(blocks, seen)]
        for rows, n, s in zip(blocks, seen, scores):
            if diag_at is None:
                m_prev = m_ref[rows, :]
                m_new = jnp.maximum(m_prev, jnp.max(s, axis=-1, keepdims=True))
                p = jnp.exp2(s - m_new).astype(BF16)
                acc_ref[rows, :] = jnp.exp2(m_prev - m_new) * acc_ref[rows, :] + _dot(p, v[:n])
            else:
                tail = s[:, diag_at:] + bias_ref[rows, :n - diag_at]
                s = tail if diag_at == 0 else jnp.concatenate([s[:, :diag_at], tail], axis=1)
                m_new = jnp.max(s, axis=-1, keepdims=True)
                acc_ref[rows, :] = _dot(jnp.exp2(s - m_new).astype(BF16), v[:n])
            m_ref[rows, :] = m_new

    @pl.when(i == 0)
    def _():
        step(0, tile, diag_at=0)

    @pl.when(i > 0)
    def _():
        step((i - 1) * tile, 2 * tile, diag_at=tile)

    rest = jnp.maximum(i - 1, 0)

    def body(j, carry):
        step(j * (2 * tile), 2 * tile)
        return carry

    lax.fori_loop(0, rest // 2, body, 0)

    @pl.when(rest % 2 == 1)
    def _():
        step((rest - 1) * tile, tile)
    acc = acc_ref[...]
    o_ref[...] = (acc[:, :HEAD_DIM] / acc[:, HEAD_DIM:]).astype(o_ref.dtype)


def _attention(q, k, v, batch, seq):
    t = q.shape[0]
    tile = ATTN_TILE
    nq = seq // tile
    return pl.pallas_call(
        functools.partial(_attn_kernel, tile=tile),
        out_shape=jax.ShapeDtypeStruct((t, HEADS * HEAD_DIM), BF16),
        grid=(batch, HEADS, nq),
        in_specs=[pl.BlockSpec((tile, 2 * HEAD_DIM), lambda b, h, i: (b * nq + i, h)),
                  pl.BlockSpec((seq, 2 * HEAD_DIM), lambda b, h, i: (b, h)),
                  pl.BlockSpec((seq, 2 * HEAD_DIM), lambda b, h, i: (b, h))],
        out_specs=pl.BlockSpec((tile, HEAD_DIM), lambda b, h, i: (b * nq + i, h)),
        scratch_shapes=[pltpu.VMEM((tile, 1), F32), pltpu.VMEM((tile, 2 * HEAD_DIM), F32),
                        pltpu.VMEM((tile, tile), F32)],
        compiler_params=_params("arbitrary", "arbitrary", "arbitrary"),
        name="mla_attention",
    )(q, k, v)


def _post_kernel(o_ref, x_ref, mod_ref, g_ref, wo_ref, w1_ref, w2_ref, out_ref, h1_ref, *, ff_tile):
    mod = mod_ref[0]
    g = g_ref[...]
    d_ff = w1_ref.shape[1]
    blocks = [slice(r * POST_ROWS, (r + 1) * POST_ROWS) for r in range(x_ref.shape[0] // POST_ROWS)]
    ys = [_dot(o_ref[rows, :], wo_ref[...]) for rows in blocks]
    x1s = []
    for rows, y in zip(blocks, ys):
        x1 = x_ref[rows, :] + mod[2:3] * (_rms(y) * g[1:2])
        x1s.append(x1)
        h = (_rms(x1) * g[2:3] * (1.0 + mod[4:5]) + mod[3:4]).astype(BF16)
        for j in range(d_ff // ff_tile):
            a = jnp.maximum(_dot(h, w1_ref[:, j * ff_tile:(j + 1) * ff_tile]), 0.0)
            h1_ref[rows, j * ff_tile:(j + 1) * ff_tile] = (a * a).astype(BF16)
    for rows, x1 in zip(blocks, x1s):
        y2 = _dot(h1_ref[rows, :], w2_ref[...])
        out_ref[rows, :] = x1 + mod[5:6] * (_rms(y2) * g[3:4])


def _post(o, x, mod, g4, wo, w1, w2, seq):
    t, d = x.shape
    tm = TOKEN_TILE
    per_seq = seq // tm
    d_ff = w1.shape[1]
    row = lambda i: (i, 0)
    return pl.pallas_call(
        functools.partial(_post_kernel, ff_tile=1024),
        out_shape=jax.ShapeDtypeStruct((t, d), F32),
        grid=(t // tm,),
        in_specs=[pl.BlockSpec((tm, o.shape[1]), row),
                  pl.BlockSpec((tm, d), row),
                  pl.BlockSpec((1, 6, d), lambda i: (i // per_seq, 0, 0)),
                  _resident((4, d)),
                  _resident(wo.shape), _resident(w1.shape), _resident(w2.shape)],
        out_specs=pl.BlockSpec((tm, d), row),
        scratch_shapes=[pltpu.VMEM((tm, d_ff), BF16)],
        compiler_params=_params("arbitrary"),
        name="post_ffn",
    )(o, x, mod, g4, wo, w1, w2)


def _gdn_pre_kernel(x_ref, mod_ref, g_ref, win_ref, conv_ref, gate_ref,
                    q_ref, k_ref, v_ref, z_ref, gates_ref, buf_ref, *, per_seq):
    tm = x_ref.shape[0]
    d_qkv = 3 * HEADS * HEAD_DIM
    d_head = HEADS * HEAD_DIM
    halo = 8
    mod = mod_ref[0]
    gp = gate_ref[...]

    @pl.when(pl.program_id(0) % per_seq == 0)
    def _():
        buf_ref[0:halo, :] = jnp.zeros((halo, d_qkv), F32)

    blocks = [slice(r * PRE_ROWS, (r + 1) * PRE_ROWS) for r in range(tm // PRE_ROWS)]
    projs = []
    for rows in blocks:
        h = _rms(x_ref[rows, :]) * g_ref[...] * (1.0 + mod[1:2]) + mod[0:1]
        projs.append(_dot(h.astype(BF16), win_ref[...]))
    for rows, proj in zip(blocks, projs):
        buf_ref[halo + rows.start:halo + rows.stop, :] = proj[:, :d_qkv]
    last_rows = buf_ref[tm:tm + halo, :]

    for rows, proj in zip(blocks, projs):
        ext = buf_ref[rows.start:rows.start + halo + PRE_ROWS, :]
        conv = conv_ref[CONV_TAPS - 1:CONV_TAPS, :] * ext[halo:]
        for tap in range(CONV_TAPS - 1):
            conv = conv + conv_ref[tap:tap + 1, :] * pltpu.roll(ext, CONV_TAPS - 1 - tap, 0)[halo:]
        act = conv * jax.nn.sigmoid(conv)

        for hd in range(HEADS):
            lo = hd * HEAD_DIM
            qh = act[:, lo:lo + HEAD_DIM]
            kh = act[:, d_head + lo:d_head + lo + HEAD_DIM]
            qn = qh * lax.rsqrt(jnp.sum(qh * qh, axis=-1, keepdims=True) + 1e-6)
            q_ref[rows, lo:lo + HEAD_DIM] = (qn * (HEAD_DIM ** -0.5)).astype(BF16)
            k_ref[rows, lo:lo + HEAD_DIM] = (
                kh * lax.rsqrt(jnp.sum(kh * kh, axis=-1, keepdims=True) + 1e-6)).astype(BF16)
        v_ref[rows, :] = act[:, 2 * d_head:].astype(BF16)
        z = proj[:, d_qkv:d_qkv + d_head]
        z_ref[rows, :] = (z * jax.nn.sigmoid(z)).astype(BF16)

        raw = proj[:, d_qkv + d_head:]
        beta = jax.nn.sigmoid(raw)
        gl = gp[0:1] * jax.nn.softplus(raw + gp[1:2])
        pos = lax.broadcasted_iota(jnp.int32, (PRE_ROWS, LANES), 0) % CHUNK
        shift = 1
        while shift < CHUNK:
            gl = gl + jnp.where(pos >= shift, pltpu.roll(gl, shift, 0), 0.0)
            shift *= 2
        lane = lax.broadcasted_iota(jnp.int32, (PRE_ROWS, LANES), 1)
        gates_ref[rows, :] = jnp.where(lane < HEADS, beta, gl)
    buf_ref[0:halo, :] = last_rows


def _gdn_weights(w_in, a_log, dt_bias):
    d_head = HEADS * HEAD_DIM
    d_qkv = 3 * d_head
    pad = jnp.zeros((w_in.shape[0], LANES - 2 * HEADS), F32)
    win = jnp.concatenate([w_in[:, :d_qkv], w_in[:, d_qkv + 2 * HEADS:],
                           w_in[:, d_qkv:d_qkv + 2 * HEADS], pad], axis=1).astype(BF16)
    zeros = jnp.zeros((HEADS,), F32)
    lane_pad = jnp.zeros((LANES - 2 * HEADS,), F32)
    neg_a = jnp.concatenate([zeros, -jnp.exp(a_log.astype(F32)), lane_pad])
    dtb = jnp.concatenate([zeros, dt_bias.astype(F32), lane_pad])
    return win, jnp.stack([neg_a, dtb])


def _gdn_pre(x, mod, g, win, conv_w, gate_p, seq):
    t, d = x.shape
    tm = TOKEN_TILE
    per_seq = seq // tm
    d_head = HEADS * HEAD_DIM
    row = lambda i: (i, 0)
    tok = jax.ShapeDtypeStruct((t, d_head), BF16)
    return pl.pallas_call(
        functools.partial(_gdn_pre_kernel, per_seq=per_seq),
        out_shape=(tok, tok, tok, tok, jax.ShapeDtypeStruct((t, LANES), F32)),
        grid=(t // tm,),
        in_specs=[pl.BlockSpec((tm, d), row),
                  pl.BlockSpec((1, 6, d), lambda i: (i // per_seq, 0, 0)),
                  _resident((1, d)), _resident(win.shape), _resident(conv_w.shape),
                  _resident((2, LANES))],
        out_specs=(pl.BlockSpec((tm, d_head), row),) * 4 + (pl.BlockSpec((tm, LANES), row),),
        scratch_shapes=[pltpu.VMEM((tm + 8, 3 * d_head), F32)],
        compiler_params=_params("arbitrary"),
        name="gdn_pre",
    )(x, mod, g.reshape(1, d), win, conv_w, gate_p)


def _gdn_core_kernel(q_ref, k_ref, v_ref, z_ref, gates_ref, gt_ref, gout_ref, o_ref, state_ref):
    seg = q_ref.shape[0]
    n_chunks = seg // CHUNK
    n_tiles = seg // LANES
    heads = range(HEADS)

    @pl.when(pl.program_id(1) == 0)
    def _():
        state_ref[...] = jnp.zeros(state_ref.shape, F32)

    row = lax.broadcasted_iota(jnp.int32, (seg, seg), 0)
    col = lax.broadcasted_iota(jnp.int32, (seg, seg), 1)
    same = (row // CHUNK) == (col // CHUNK)
    lower = jnp.where(same & (row >= col), 1.0, 0.0)
    strict = jnp.where(same & (row > col), 1.0, 0.0)
    blocks = jnp.where(same, 1.0, 0.0)
    gout = gout_ref[...]

    t_row = lax.broadcasted_iota(jnp.int32, (seg, LANES), 0)
    t_lane = lax.broadcasted_iota(jnp.int32, (seg, LANES), 1)
    t_own = (t_lane // CHUNK) == ((t_row // CHUNK) % 2)
    t_i, t_j = t_row % CHUNK, t_lane % CHUNK
    w_i = lax.broadcasted_iota(jnp.int32, (CHUNK, seg), 0)
    w_lane = lax.broadcasted_iota(jnp.int32, (CHUNK, seg), 1)
    w_chunk, w_j = w_lane // CHUNK, w_lane % CHUNK
    lane_chunk = lax.broadcasted_iota(jnp.int32, (1, seg), 1) // CHUNK

    def tall(full):
        t = full[:, :LANES]
        for i in range(1, n_tiles):
            t = jnp.where(t_row // LANES == i, full[:, i * LANES:(i + 1) * LANES], t)
        return jnp.where(t_own, t, 0.0)

    def wide(full):
        w = full[:CHUNK]
        for c in range(1, n_chunks):
            w = jnp.where(w_chunk == c, full[c * CHUNK:(c + 1) * CHUNK], w)
        return w

    def left(t):
        return t.astype(BF16)

    def right(w):
        w = w.astype(BF16)
        return jnp.concatenate([w, w], axis=0)

    cols = [slice(hd * HEAD_DIM, (hd + 1) * HEAD_DIM) for hd in heads]
    g_row, low, a_qk, rhs, q_dec, kd_t = [], [], [], [], [], []
    for hd in heads:
        k_b = k_ref[:, cols[hd]]
        q_b = q_ref[:, cols[hd]]
        kf = k_b.astype(F32)
        beta = gates_ref[:, hd:hd + 1]
        g_col = gates_ref[:, HEADS + hd:HEADS + hd + 1]
        g_row.append(gt_ref[HEADS + hd:HEADS + hd + 1, :])
        decay = jnp.exp(jnp.minimum(g_col - g_row[hd], 0.0))
        kb = kf * beta
        low.append(_dot_nt(kb.astype(BF16), k_b) * (decay * strict))
        a_qk.append((_dot_nt(q_b, k_b) * (decay * lower)).astype(BF16))
        e_col = jnp.exp(g_col)
        rhs.append(jnp.concatenate([v_ref[:, cols[hd]].astype(F32) * beta, kb * e_col], axis=1))
        q_dec.append(q_b.astype(F32) * e_col)
        g_end = g_row[hd][:, CHUNK - 1:CHUNK]
        for c in range(1, n_chunks):
            g_end = jnp.where(lane_chunk == c, g_row[hd][:, (c + 1) * CHUNK - 1:(c + 1) * CHUNK], g_end)
        kd_t.append((kf.T * jnp.exp(g_end - g_row[hd])).astype(BF16))

    low_t = [tall(low[hd]) for hd in heads]
    inv_t = [jnp.where(t_own & (t_i == t_j), 1.0, 0.0) - jnp.where(t_i // 2 == t_j // 2, low_t[hd], 0.0)
             for hd in heads]
    inv_w = [jnp.where(w_i == w_j, 1.0, 0.0) - jnp.where(w_i // 2 == w_j // 2, wide(low[hd]), 0.0)
             for hd in heads]
    size = 2
    while size < CHUNK:
        joins = (t_i // (2 * size) == t_j // (2 * size)) & (t_i // size != t_j // size)
        step = [_dot(left(jnp.where(joins, low_t[hd], 0.0)), right(inv_w[hd])) for hd in heads]
        step = [_dot(left(inv_t[hd]), right(wide(step[hd]))) for hd in heads]
        inv_t = [inv_t[hd] - tall(step[hd]) for hd in heads]
        if 2 * size < CHUNK:
            inv_w = [inv_w[hd] - wide(step[hd]) for hd in heads]
        size *= 2
    inv = [(jnp.concatenate([inv_t[hd]] * n_tiles, axis=1) * blocks).astype(BF16) for hd in heads]
    uw = [_dot(inv[hd], rhs[hd].astype(BF16)) for hd in heads]

    state = [state_ref[hd] for hd in heads]
    zero_rows = jnp.zeros((CHUNK, HEAD_DIM), BF16)
    for n in range(n_chunks):
        rows = slice(n * CHUNK, (n + 1) * CHUNK)
        tile = slice((n // 2) * LANES, (n // 2 + 1) * LANES)
        ws, mix = [], []
        for hd in heads:
            lhs = jnp.concatenate([uw[hd][rows, HEAD_DIM:], q_dec[hd][rows]], axis=0).astype(BF16)
            ws.append(_dot(lhs, state[hd].astype(BF16)))
        for hd in heads:
            v_new = (uw[hd][rows, :HEAD_DIM] - ws[hd][:CHUNK]).astype(BF16)
            v_two = jnp.concatenate([v_new, zero_rows] if n % 2 == 0 else [zero_rows, v_new], axis=0)
            lhs = jnp.concatenate([a_qk[hd][rows, tile], kd_t[hd][:, tile]], axis=0)
            mix.append(_dot(lhs, v_two))
        for hd in heads:
            g_last = g_row[hd][:, (n + 1) * CHUNK - 1:(n + 1) * CHUNK]
            state[hd] = state[hd] * jnp.exp(g_last) + mix[hd][CHUNK:]
            o = _rms(ws[hd][CHUNK:] + mix[hd][:CHUNK]) * gout * z_ref[rows, cols[hd]].astype(F32)
            o_ref[rows, cols[hd]] = o.astype(o_ref.dtype)
    for hd in heads:
        state_ref[hd] = state[hd]


def _gdn_core(q, k, v, z, gates, gates_t, gout, batch, seq):
    t, d_head = q.shape
    seg = GDN_SEG
    per_seq = seq // seg
    blk = pl.BlockSpec((seg, d_head), lambda b, s: (b * per_seq + s, 0))
    return pl.pallas_call(
        _gdn_core_kernel,
        out_shape=jax.ShapeDtypeStruct((t, d_head), BF16),
        grid=(batch, per_seq),
        in_specs=[blk, blk, blk, blk,
                  pl.BlockSpec((seg, LANES), lambda b, s: (b * per_seq + s, 0)),
                  pl.BlockSpec((2 * HEADS, seg), lambda b, s: (0, b * per_seq + s)),
                  pl.BlockSpec((1, HEAD_DIM), lambda b, s: (0, 0))],
        out_specs=blk,
        scratch_shapes=[pltpu.VMEM((HEADS, HEAD_DIM, HEAD_DIM), F32)],
        compiler_params=_params("arbitrary", "arbitrary"),
        name="gdn_core",
    )(q, k, v, z, gates, gates_t, gout.reshape(1, HEAD_DIM))


def kernel(x, c, positions, ada_w, ada_b, sandwich_g, mla_w_in, mla_q_norm_g, mla_kv_norm_g, mla_w_uq, mla_w_ukv, mla_w_o, gdn_w_in, gdn_conv_w, gdn_a_log, gdn_dt_bias, gdn_out_norm_g, gdn_w_o, ffn_w1, ffn_w2):
    batch, seq, d = x.shape
    depth = ada_w.shape[0]
    t = batch * seq
    assert seq % TOKEN_TILE == 0 and seq % ATTN_TILE == 0 and seq % GDN_SEG == 0
    mod = _adaln(c, ada_w, ada_b).reshape(depth, batch, 6, d)
    cos, sin = _rope_tables(positions)
    xs = x.reshape(t, d)
    for i in range(depth):
        j = i // 2
        if i % 2 == 0:
            dq, dkv = mla_q_norm_g.shape[-1], mla_kv_norm_g.shape[-1]
            win, wuq, wukv = _mla_weights(mla_w_in[j], mla_w_uq[j], mla_w_ukv[j], dq, dkv)
            q, k, v = _mla_pre(xs, mod[i], sandwich_g[i, 0], cos, sin, win, mla_q_norm_g[j],
                               mla_kv_norm_g[j], wuq, wukv, seq)
            o = _attention(q, k, v, batch, seq)
            wo = mla_w_o[j]
        else:
            win, gate_p = _gdn_weights(gdn_w_in[j], gdn_a_log[j], gdn_dt_bias[j])
            q, k, v, z, gates = _gdn_pre(xs, mod[i], sandwich_g[i, 0], win, gdn_conv_w[j], gate_p, seq)
            gates_t = gates[:, :2 * HEADS].T
            o = _gdn_core(q, k, v, z, gates, gates_t, gdn_out_norm_g[j], batch, seq)
            wo = gdn_w_o[j]
        xs = _post(o, xs, mod[i], sandwich_g[i], wo.astype(BF16), ffn_w1[i].astype(BF16),
                   ffn_w2[i].astype(BF16), seq)
    return xs.reshape(batch, seq, d)
```

```python
import functools
import math

import jax
import jax.numpy as jnp
from jax import lax
from jax.experimental import pallas as pl
from jax.experimental.pallas import tpu as pltpu

F32 = jnp.float32
BF16 = jnp.bfloat16

LANES = 128
CHUNK = 64
HEADS = 8
HEAD_DIM = 128
ROPE_DIM = 64
NORM_EPS = 1e-6
ROPE_THETA = 10000.0
CONV_TAPS = 4
VMEM_LIMIT = 56 * 1024 * 1024

TOKEN_TILE = 512
ATTN_TILE = 2048
ATTN_ROWS = 256
GDN_SEG = 256
POST_ROWS = 256
PRE_ROWS = 256


def _rms(x):
    return x * lax.rsqrt(jnp.mean(x * x, axis=-1, keepdims=True) + NORM_EPS)


def _dot(a, b):
    return jnp.dot(a, b, preferred_element_type=F32)


def _dot_nt(a, b):
    return lax.dot_general(a, b, (((1,), (1,)), ((), ())), preferred_element_type=F32)


def _dot_tn(a, b):
    return lax.dot_general(a, b, (((0,), (0,)), ((), ())), preferred_element_type=F32)


def _params(*sem):
    return pltpu.CompilerParams(dimension_semantics=sem, vmem_limit_bytes=VMEM_LIMIT)


def _resident(shape):
    return pl.BlockSpec(shape, lambda *_: (0,) * len(shape), pipeline_mode=pl.Buffered(1))


def _adaln_kernel(c_ref, w_ref, b_ref, o_ref):
    c = c_ref[...]
    act = (c * jax.nn.sigmoid(c)).astype(BF16)
    o_ref[0] = _dot(act, w_ref[0].astype(BF16)) + b_ref[0]


def _adaln(c, ada_w, ada_b):
    depth, d, n = ada_w.shape
    b = c.shape[0]
    tn = n // 4
    return pl.pallas_call(
        _adaln_kernel,
        out_shape=jax.ShapeDtypeStruct((depth, b, n), F32),
        grid=(depth, n // tn),
        in_specs=[pl.BlockSpec((b, d), lambda i, j: (0, 0)),
                  pl.BlockSpec((1, d, tn), lambda i, j: (i, 0, j)),
                  pl.BlockSpec((1, 1, tn), lambda i, j: (i, 0, j))],
        out_specs=pl.BlockSpec((1, b, tn), lambda i, j: (i, 0, j)),
        compiler_params=_params("arbitrary", "arbitrary"),
        name="adaln_mod",
    )(c, ada_w, ada_b.reshape(depth, 1, n))


def _rope_kernel(pos_ref, inv_ref, cos_ref, sin_ref):
    ang = pos_ref[...].astype(F32) * inv_ref[...]
    cos_ref[...] = jnp.cos(ang)
    sin_ref[...] = jnp.sin(ang)


def _rope_tables(positions):
    t = positions.size
    half = ROPE_DIM // 2
    inv_freq = ROPE_THETA ** (-jnp.arange(half, dtype=F32) / half)
    inv_row = jnp.tile(inv_freq, LANES // half).reshape(1, LANES)
    tm = 1024
    return pl.pallas_call(
        _rope_kernel,
        out_shape=(jax.ShapeDtypeStruct((t, LANES), F32),) * 2,
        grid=(t // tm,),
        in_specs=[pl.BlockSpec((tm, 1), lambda i: (i, 0)),
                  pl.BlockSpec((1, LANES), lambda i: (0, 0))],
        out_specs=(pl.BlockSpec((tm, LANES), lambda i: (i, 0)),) * 2,
        compiler_params=_params("arbitrary"),
        name="rope_tables",
    )(positions.reshape(t, 1), inv_row)


def _mla_pre_kernel(x_ref, mod_ref, g_ref, cos_ref, sin_ref, win_ref, gq_ref, gkv_ref,
                    wuq_ref, wukv_ref, q_ref, k_ref, v_ref, *, dq, dkv, scale):
    d_nope = HEADS * HEAD_DIM
    mod = mod_ref[0]
    h = _rms(x_ref[...]) * g_ref[...] * (1.0 + mod[1:2]) + mod[0:1]
    proj = _dot(h.astype(BF16), win_ref[...])
    cos = cos_ref[...]
    sin = sin_ref[...]
    k_rope = (proj[:, dq + dkv:dq + dkv + LANES] * cos
              + proj[:, dq + dkv + LANES:dq + dkv + 2 * LANES] * sin).astype(BF16)
    qn = (_rms(proj[:, :dq]) * gq_ref[...]).astype(BF16)
    qf = _dot(qn, wuq_ref[...])
    cos_h = jnp.concatenate([cos] * HEADS, axis=1)
    sin_h = jnp.concatenate([sin] * HEADS, axis=1)
    q_nope = (qf[:, :d_nope] * scale).astype(BF16)
    q_rope = ((qf[:, d_nope:2 * d_nope] * cos_h + qf[:, 2 * d_nope:] * sin_h) * scale).astype(BF16)
    kvn = (_rms(proj[:, dq:dq + dkv]) * gkv_ref[...]).astype(BF16)
    kv = _dot(kvn, wukv_ref[...])
    k_nope = kv[:, :d_nope].astype(BF16)
    v = kv[:, d_nope:].astype(BF16)
    ones = jnp.ones((v.shape[0], HEAD_DIM), BF16)
    for hd in range(HEADS):
        lo = hd * HEAD_DIM
        q_ref[:, 2 * lo:2 * lo + HEAD_DIM] = q_nope[:, lo:lo + HEAD_DIM]
        q_ref[:, 2 * lo + HEAD_DIM:2 * lo + 2 * HEAD_DIM] = q_rope[:, lo:lo + HEAD_DIM]
        k_ref[:, 2 * lo:2 * lo + HEAD_DIM] = k_nope[:, lo:lo + HEAD_DIM]
        k_ref[:, 2 * lo + HEAD_DIM:2 * lo + 2 * HEAD_DIM] = k_rope
        v_ref[:, 2 * lo:2 * lo + HEAD_DIM] = v[:, lo:lo + HEAD_DIM]
        v_ref[:, 2 * lo + HEAD_DIM:2 * lo + 2 * HEAD_DIM] = ones


def _swap_rot(w):
    half = ROPE_DIM // 2
    return jnp.concatenate([-w[..., half:], w[..., :half]], axis=-1)


def _mla_weights(w_in, w_uq, w_ukv, dq, dkv):
    d = w_in.shape[0]
    wkr = w_in[:, dq + dkv:]
    wkr_sw = _swap_rot(wkr)
    win = jnp.concatenate([w_in[:, :dq + dkv], wkr, wkr, wkr_sw, wkr_sw], axis=1).astype(BF16)
    wq = w_uq.reshape(dq, HEADS, HEAD_DIM + ROPE_DIM)
    nope = wq[:, :, :HEAD_DIM].reshape(dq, HEADS * HEAD_DIM)
    rope = wq[:, :, HEAD_DIM:]
    pad = jnp.zeros((dq, HEADS, HEAD_DIM - ROPE_DIM), F32)
    rope_p = jnp.concatenate([rope, pad], axis=-1).reshape(dq, HEADS * HEAD_DIM)
    rope_sw = jnp.concatenate([_swap_rot(rope), pad], axis=-1).reshape(dq, HEADS * HEAD_DIM)
    wuq = jnp.concatenate([nope, rope_p, rope_sw], axis=1).astype(BF16)
    wkv = w_ukv.reshape(dkv, HEADS, 2, HEAD_DIM).transpose(0, 2, 1, 3).reshape(dkv, 2 * HEADS * HEAD_DIM)
    del d
    return win, wuq, wkv.astype(BF16)


def _mla_pre(x, mod, g, cos, sin, win, gq, gkv, wuq, wukv, seq):
    t, d = x.shape
    dq, dkv = gq.shape[-1], gkv.shape[-1]
    tm = TOKEN_TILE
    per_seq = seq // tm
    d_nope = HEADS * HEAD_DIM
    scale = float((HEAD_DIM + ROPE_DIM) ** -0.5 * math.log2(math.e))
    row = lambda i: (i, 0)
    return pl.pallas_call(
        functools.partial(_mla_pre_kernel, dq=dq, dkv=dkv, scale=scale),
        out_shape=(jax.ShapeDtypeStruct((t, 2 * d_nope), BF16),
                   jax.ShapeDtypeStruct((t, 2 * d_nope), BF16),
                   jax.ShapeDtypeStruct((t, 2 * d_nope), BF16)),
        grid=(t // tm,),
        in_specs=[pl.BlockSpec((tm, d), row),
                  pl.BlockSpec((1, 6, d), lambda i: (i // per_seq, 0, 0)),
                  _resident((1, d)),
                  pl.BlockSpec((tm, LANES), row),
                  pl.BlockSpec((tm, LANES), row),
                  _resident(win.shape), _resident((1, dq)), _resident((1, dkv)),
                  _resident(wuq.shape), _resident(wukv.shape)],
        out_specs=(pl.BlockSpec((tm, 2 * d_nope), row),
                   pl.BlockSpec((tm, 2 * d_nope), row),
                   pl.BlockSpec((tm, 2 * d_nope), row)),
        compiler_params=_params("arbitrary"),
        name="mla_pre",
    )(x, mod, g.reshape(1, d), cos, sin, win, gq.reshape(1, dq), gkv.reshape(1, dkv), wuq, wukv)


def _attn_kernel(q_ref, k_ref, v_ref, o_ref, m_ref, acc_ref, bias_ref, *, tile):
    i = pl.program_id(2)

    @pl.when((pl.program_id(0) == 0) & (pl.program_id(1) == 0) & (i == 0))
    def _():
        qc = lax.broadcasted_iota(jnp.int32, (ATTN_ROWS, ATTN_ROWS), 0) // CHUNK
        kc = lax.broadcasted_iota(jnp.int32, (ATTN_ROWS, ATTN_ROWS), 1) // CHUNK
        bias_ref[...] = jnp.where(kc <= qc, 0.0, -jnp.inf)

    def step(j, diagonal):
        start = pl.multiple_of(j * tile, tile)
        k = k_ref[pl.ds(start, tile), :]
        v = v_ref[pl.ds(start, tile), :]
        blocks = [slice(r * ATTN_ROWS, (r + 1) * ATTN_ROWS) for r in range(tile // ATTN_ROWS)]
        if diagonal:
            blocks = blocks[::-1]
        seen = [rows.stop if diagonal else tile for rows in blocks]
        scores = [_dot_nt(q_ref[rows, :], k[:n]) for rows, n in zip(blocks, seen)]
        for rows, n, s in zip(blocks, seen, scores):
            if diagonal:
                own = s[:, rows.start:] + bias_ref[...]
                s = own if rows.start == 0 else jnp.concatenate([s[:, :rows.start], own], axis=1)
                m_new = jnp.max(s, axis=-1, keepdims=True)
                acc_ref[rows, :] = _dot(jnp.exp2(s - m_new).astype(BF16), v[:n])
            else:
                m_prev = m_ref[rows, :]
                m_new = jnp.maximum(m_prev, jnp.max(s, axis=-1, keepdims=True))
                p = jnp.exp2(s - m_new).astype(BF16)
                acc_ref[rows, :] = jnp.exp2(m_prev - m_new) * acc_ref[rows, :] + _dot(p, v[:n])
            m_ref[rows, :] = m_new

    def body(j, carry):
        step(j, False)
        return carry

    step(i, True)
    lax.fori_loop(0, i, body, 0)
    acc = acc_ref[...]
    o_ref[...] = (acc[:, :HEAD_DIM] / acc[:, HEAD_DIM:]).astype(o_ref.dtype)


def _attention(q, k, v, batch, seq):
    t = q.shape[0]
    tile = ATTN_TILE
    nq = seq // tile
    return pl.pallas_call(
        functools.partial(_attn_kernel, tile=tile),
        out_shape=jax.ShapeDtypeStruct((t, HEADS * HEAD_DIM), BF16),
        grid=(batch, HEADS, nq),
        in_specs=[pl.BlockSpec((tile, 2 * HEAD_DIM), lambda b, h, i: (b * nq + i, h)),
                  pl.BlockSpec((seq, 2 * HEAD_DIM), lambda b, h, i: (b, h)),
                  pl.BlockSpec((seq, 2 * HEAD_DIM), lambda b, h, i: (b, h))],
        out_specs=pl.BlockSpec((tile, HEAD_DIM), lambda b, h, i: (b * nq + i, h)),
        scratch_shapes=[pltpu.VMEM((tile, 1), F32), pltpu.VMEM((tile, 2 * HEAD_DIM), F32),
                        pltpu.VMEM((ATTN_ROWS, ATTN_ROWS), F32)],
        compiler_params=_params("arbitrary", "arbitrary", "arbitrary"),
        name="mla_attention",
    )(q, k, v)


def _post_kernel(o_ref, x_ref, mod_ref, g_ref, wo_ref, w1_ref, w2_ref, out_ref, h1_ref, *, ff_tile):
    mod = mod_ref[0]
    g = g_ref[...]
    d_ff = w1_ref.shape[1]
    blocks = [slice(r * POST_ROWS, (r + 1) * POST_ROWS) for r in range(x_ref.shape[0] // POST_ROWS)]
    ys = [_dot(o_ref[rows, :], wo_ref[...]) for rows in blocks]
    x1s = []
    for rows, y in zip(blocks, ys):
        x1 = x_ref[rows, :] + mod[2:3] * (_rms(y) * g[1:2])
        x1s.append(x1)
        h = (_rms(x1) * g[2:3] * (1.0 + mod[4:5]) + mod[3:4]).astype(BF16)
        for j in range(d_ff // ff_tile):
            a = jnp.maximum(_dot(h, w1_ref[:, j * ff_tile:(j + 1) * ff_tile]), 0.0)
            h1_ref[rows, j * ff_tile:(j + 1) * ff_tile] = (a * a).astype(BF16)
    for rows, x1 in zip(blocks, x1s):
        y2 = _dot(h1_ref[rows, :], w2_ref[...])
        out_ref[rows, :] = x1 + mod[5:6] * (_rms(y2) * g[3:4])


def _post(o, x, mod, g4, wo, w1, w2, seq):
    t, d = x.shape
    tm = TOKEN_TILE
    per_seq = seq // tm
    d_ff = w1.shape[1]
    row = lambda i: (i, 0)
    return pl.pallas_call(
        functools.partial(_post_kernel, ff_tile=1024),
        out_shape=jax.ShapeDtypeStruct((t, d), F32),
        grid=(t // tm,),
        in_specs=[pl.BlockSpec((tm, o.shape[1]), row),
                  pl.BlockSpec((tm, d), row),
                  pl.BlockSpec((1, 6, d), lambda i: (i // per_seq, 0, 0)),
                  _resident((4, d)),
                  _resident(wo.shape), _resident(w1.shape), _resident(w2.shape)],
        out_specs=pl.BlockSpec((tm, d), row),
        scratch_shapes=[pltpu.VMEM((tm, d_ff), BF16)],
        compiler_params=_params("arbitrary"),
        name="post_ffn",
    )(o, x, mod, g4, wo, w1, w2)


def _gdn_pre_kernel(x_ref, mod_ref, g_ref, win_ref, conv_ref, gate_ref,
                    q_ref, k_ref, v_ref, z_ref, gates_ref, buf_ref, *, per_seq):
    tm = x_ref.shape[0]
    d_qkv = 3 * HEADS * HEAD_DIM
    d_head = HEADS * HEAD_DIM
    halo = 8
    mod = mod_ref[0]
    gp = gate_ref[...]

    @pl.when(pl.program_id(0) % per_seq == 0)
    def _():
        buf_ref[0:halo, :] = jnp.zeros((halo, d_qkv), F32)

    blocks = [slice(r * PRE_ROWS, (r + 1) * PRE_ROWS) for r in range(tm // PRE_ROWS)]
    projs = []
    for rows in blocks:
        h = _rms(x_ref[rows, :]) * g_ref[...] * (1.0 + mod[1:2]) + mod[0:1]
        projs.append(_dot(h.astype(BF16), win_ref[...]))
    for rows, proj in zip(blocks, projs):
        buf_ref[halo + rows.start:halo + rows.stop, :] = proj[:, :d_qkv]
    last_rows = buf_ref[tm:tm + halo, :]

    for rows, proj in zip(blocks, projs):
        ext = buf_ref[rows.start:rows.start + halo + PRE_ROWS, :]
        conv = conv_ref[CONV_TAPS - 1:CONV_TAPS, :] * ext[halo:]
        for tap in range(CONV_TAPS - 1):
            conv = conv + conv_ref[tap:tap + 1, :] * pltpu.roll(ext, CONV_TAPS - 1 - tap, 0)[halo:]
        act = conv * jax.nn.sigmoid(conv)

        for hd in range(HEADS):
            lo = hd * HEAD_DIM
            qh = act[:, lo:lo + HEAD_DIM]
            kh = act[:, d_head + lo:d_head + lo + HEAD_DIM]
            qn = qh * lax.rsqrt(jnp.sum(qh * qh, axis=-1, keepdims=True) + 1e-6)
            q_ref[rows, lo:lo + HEAD_DIM] = (qn * (HEAD_DIM ** -0.5)).astype(BF16)
            k_ref[rows, lo:lo + HEAD_DIM] = (
                kh * lax.rsqrt(jnp.sum(kh * kh, axis=-1, keepdims=True) + 1e-6)).astype(BF16)
        v_ref[rows, :] = act[:, 2 * d_head:].astype(BF16)
        z = proj[:, d_qkv:d_qkv + d_head]
        z_ref[rows, :] = (z * jax.nn.sigmoid(z)).astype(BF16)

        raw = proj[:, d_qkv + d_head:]
        beta = jax.nn.sigmoid(raw)
        gl = gp[0:1] * jax.nn.softplus(raw + gp[1:2])
        pos = lax.broadcasted_iota(jnp.int32, (PRE_ROWS, LANES), 0) % CHUNK
        shift = 1
        while shift < CHUNK:
            gl = gl + jnp.where(pos >= shift, pltpu.roll(gl, shift, 0), 0.0)
            shift *= 2
        lane = lax.broadcasted_iota(jnp.int32, (PRE_ROWS, LANES), 1)
        gates_ref[rows, :] = jnp.where(lane < HEADS, beta, gl)
    buf_ref[0:halo, :] = last_rows


def _gdn_weights(w_in, a_log, dt_bias):
    d_head = HEADS * HEAD_DIM
    d_qkv = 3 * d_head
    pad = jnp.zeros((w_in.shape[0], LANES - 2 * HEADS), F32)
    win = jnp.concatenate([w_in[:, :d_qkv], w_in[:, d_qkv + 2 * HEADS:],
                           w_in[:, d_qkv:d_qkv + 2 * HEADS], pad], axis=1).astype(BF16)
    zeros = jnp.zeros((HEADS,), F32)
    lane_pad = jnp.zeros((LANES - 2 * HEADS,), F32)
    neg_a = jnp.concatenate([zeros, -jnp.exp(a_log.astype(F32)), lane_pad])
    dtb = jnp.concatenate([zeros, dt_bias.astype(F32), lane_pad])
    return win, jnp.stack([neg_a, dtb])


def _gdn_pre(x, mod, g, win, conv_w, gate_p, seq):
    t, d = x.shape
    tm = TOKEN_TILE
    per_seq = seq // tm
    d_head = HEADS * HEAD_DIM
    row = lambda i: (i, 0)
    tok = jax.ShapeDtypeStruct((t, d_head), BF16)
    return pl.pallas_call(
        functools.partial(_gdn_pre_kernel, per_seq=per_seq),
        out_shape=(tok, tok, tok, tok, jax.ShapeDtypeStruct((t, LANES), F32)),
        grid=(t // tm,),
        in_specs=[pl.BlockSpec((tm, d), row),
                  pl.BlockSpec((1, 6, d), lambda i: (i // per_seq, 0, 0)),
                  _resident((1, d)), _resident(win.shape), _resident(conv_w.shape),
                  _resident((2, LANES))],
        out_specs=(pl.BlockSpec((tm, d_head), row),) * 4 + (pl.BlockSpec((tm, LANES), row),),
        scratch_shapes=[pltpu.VMEM((tm + 8, 3 * d_head), F32)],
        compiler_params=_params("arbitrary"),
        name="gdn_pre",
    )(x, mod, g.reshape(1, d), win, conv_w, gate_p)


def _gdn_core_kernel(q_ref, k_ref, v_ref, z_ref, gates_ref, gt_ref, gout_ref, o_ref, state_ref):
    seg = q_ref.shape[0]
    n_chunks = seg // CHUNK
    n_tiles = seg // LANES
    heads = range(HEADS)

    @pl.when(pl.program_id(1) == 0)
    def _():
        state_ref[...] = jnp.zeros(state_ref.shape, F32)

    row = lax.broadcasted_iota(jnp.int32, (seg, seg), 0)
    col = lax.broadcasted_iota(jnp.int32, (seg, seg), 1)
    same = (row // CHUNK) == (col // CHUNK)
    lower = jnp.where(same & (row >= col), 1.0, 0.0)
    strict = jnp.where(same & (row > col), 1.0, 0.0)
    blocks = jnp.where(same, 1.0, 0.0)
    gout = gout_ref[...]

    t_row = lax.broadcasted_iota(jnp.int32, (seg, LANES), 0)
    t_lane = lax.broadcasted_iota(jnp.int32, (seg, LANES), 1)
    t_own = (t_lane // CHUNK) == ((t_row // CHUNK) % 2)
    t_i, t_j = t_row % CHUNK, t_lane % CHUNK
    w_i = lax.broadcasted_iota(jnp.int32, (CHUNK, seg), 0)
    w_lane = lax.broadcasted_iota(jnp.int32, (CHUNK, seg), 1)
    w_chunk, w_j = w_lane // CHUNK, w_lane % CHUNK
    lane_chunk = lax.broadcasted_iota(jnp.int32, (1, seg), 1) // CHUNK

    def tall(full):
        t = full[:, :LANES]
        for i in range(1, n_tiles):
            t = jnp.where(t_row // LANES == i, full[:, i * LANES:(i + 1) * LANES], t)
        return jnp.where(t_own, t, 0.0)

    def wide(full):
        w = full[:CHUNK]
        for c in range(1, n_chunks):
            w = jnp.where(w_chunk == c, full[c * CHUNK:(c + 1) * CHUNK], w)
        return w

    def left(t):
        return t.astype(BF16)

    def right(w):
        w = w.astype(BF16)
        return jnp.concatenate([w, w], axis=0)

    cols = [slice(hd * HEAD_DIM, (hd + 1) * HEAD_DIM) for hd in heads]
    g_row, low, a_qk, rhs, q_dec, kd_t = [], [], [], [], [], []
    for hd in heads:
        k_b = k_ref[:, cols[hd]]
        q_b = q_ref[:, cols[hd]]
        kf = k_b.astype(F32)
        beta = gates_ref[:, hd:hd + 1]
        g_col = gates_ref[:, HEADS + hd:HEADS + hd + 1]
        g_row.append(gt_ref[HEADS + hd:HEADS + hd + 1, :])
        decay = jnp.exp(jnp.minimum(g_col - g_row[hd], 0.0))
        kb = kf * beta
        low.append(_dot_nt(kb.astype(BF16), k_b) * (decay * strict))
        a_qk.append((_dot_nt(q_b, k_b) * (decay * lower)).astype(BF16))
        e_col = jnp.exp(g_col)
        rhs.append(jnp.concatenate([v_ref[:, cols[hd]].astype(F32) * beta, kb * e_col], axis=1))
        q_dec.append(q_b.astype(F32) * e_col)
        g_end = g_row[hd][:, CHUNK - 1:CHUNK]
        for c in range(1, n_chunks):
            g_end = jnp.where(lane_chunk == c, g_row[hd][:, (c + 1) * CHUNK - 1:(c + 1) * CHUNK], g_end)
        kd_t.append((kf.T * jnp.exp(g_end - g_row[hd])).astype(BF16))

    low_t = [tall(low[hd]) for hd in heads]
    inv_t = [jnp.where(t_own & (t_i == t_j), 1.0, 0.0) - jnp.where(t_i // 2 == t_j // 2, low_t[hd], 0.0)
             for hd in heads]
    inv_w = [jnp.where(w_i == w_j, 1.0, 0.0) - jnp.where(w_i // 2 == w_j // 2, wide(low[hd]), 0.0)
             for hd in heads]
    size = 2
    while size < CHUNK:
        joins = (t_i // (2 * size) == t_j // (2 * size)) & (t_i // size != t_j // size)
        step = [_dot(left(jnp.where(joins, low_t[hd], 0.0)), right(inv_w[hd])) for hd in heads]
        step = [_dot(left(inv_t[hd]), right(wide(step[hd]))) for hd in heads]
        inv_t = [inv_t[hd] - tall(step[hd]) for hd in heads]
        if 2 * size < CHUNK:
            inv_w = [inv_w[hd] - wide(step[hd]) for hd in heads]
        size *= 2
    inv = [(jnp.concatenate([inv_t[hd]] * n_tiles, axis=1) * blocks).astype(BF16) for hd in heads]
    uw = [_dot(inv[hd], rhs[hd].astype(BF16)) for hd in heads]

    state = [state_ref[hd] for hd in heads]
    zero_rows = jnp.zeros((CHUNK, HEAD_DIM), BF16)
    for n in range(n_chunks):
        rows = slice(n * CHUNK, (n + 1) * CHUNK)
        tile = slice((n // 2) * LANES, (n // 2 + 1) * LANES)
        ws, mix = [], []
        for hd in heads:
            lhs = jnp.concatenate([uw[hd][rows, HEAD_DIM:], q_dec[hd][rows]], axis=0).astype(BF16)
            ws.append(_dot(lhs, state[hd].astype(BF16)))
        for hd in heads:
            v_new = (uw[hd][rows, :HEAD_DIM] - ws[hd][:CHUNK]).astype(BF16)
            v_two = jnp.concatenate([v_new, zero_rows] if n % 2 == 0 else [zero_rows, v_new], axis=0)
            lhs = jnp.concatenate([a_qk[hd][rows, tile], kd_t[hd][:, tile]], axis=0)
            mix.append(_dot(lhs, v_two))
        for hd in heads:
            g_last = g_row[hd][:, (n + 1) * CHUNK - 1:(n + 1) * CHUNK]
            state[hd] = state[hd] * jnp.exp(g_last) + mix[hd][CHUNK:]
            o = _rms(ws[hd][CHUNK:] + mix[hd][:CHUNK]) * gout * z_ref[rows, cols[hd]].astype(F32)
            o_ref[rows, cols[hd]] = o.astype(o_ref.dtype)
    for hd in heads:
        state_ref[hd] = state[hd]


def _gdn_core(q, k, v, z, gates, gates_t, gout, batch, seq):
    t, d_head = q.shape
    seg = GDN_SEG
    per_seq = seq // seg
    blk = pl.BlockSpec((seg, d_head), lambda b, s: (b * per_seq + s, 0))
    return pl.pallas_call(
        _gdn_core_kernel,
        out_shape=jax.ShapeDtypeStruct((t, d_head), BF16),
        grid=(batch, per_seq),
        in_specs=[blk, blk, blk, blk,
                  pl.BlockSpec((seg, LANES), lambda b, s: (b * per_seq + s, 0)),
                  pl.BlockSpec((2 * HEADS, seg), lambda b, s: (0, b * per_seq + s)),
                  pl.BlockSpec((1, HEAD_DIM), lambda b, s: (0, 0))],
        out_specs=blk,
        scratch_shapes=[pltpu.VMEM((HEADS, HEAD_DIM, HEAD_DIM), F32)],
        compiler_params=_params("arbitrary", "arbitrary"),
        name="gdn_core",
    )(q, k, v, z, gates, gates_t, gout.reshape(1, HEAD_DIM))


def kernel(x, c, positions, ada_w, ada_b, sandwich_g, mla_w_in, mla_q_norm_g, mla_kv_norm_g, mla_w_uq, mla_w_ukv, mla_w_o, gdn_w_in, gdn_conv_w, gdn_a_log, gdn_dt_bias, gdn_out_norm_g, gdn_w_o, ffn_w1, ffn_w2):
    batch, seq, d = x.shape
    depth = ada_w.shape[0]
    t = batch * seq
    assert seq % TOKEN_TILE == 0 and seq % ATTN_TILE == 0 and seq % GDN_SEG == 0
    mod = _adaln(c, ada_w, ada_b).reshape(depth, batch, 6, d)
    cos, sin = _rope_tables(positions)
    xs = x.reshape(t, d)
    for i in range(depth):
        j = i // 2
        if i % 2 == 0:
            dq, dkv = mla_q_norm_g.shape[-1], mla_kv_norm_g.shape[-1]
            win, wuq, wukv = _mla_weights(mla_w_in[j], mla_w_uq[j], mla_w_ukv[j], dq, dkv)
            q, k, v = _mla_pre(xs, mod[i], sandwich_g[i, 0], cos, sin, win, mla_q_norm_g[j],
                               mla_kv_norm_g[j], wuq, wukv, seq)
            o = _attention(q, k, v, batch, seq)
            wo = mla_w_o[j]
        else:
            win, gate_p = _gdn_weights(gdn_w_in[j], gdn_a_log[j], gdn_dt_bias[j])
            q, k, v, z, gates = _gdn_pre(xs, mod[i], sandwich_g[i, 0], win, gdn_conv_w[j], gate_p, seq)
            gates_t = gates[:, :2 * HEADS].T
            o = _gdn_core(q, k, v, z, gates, gates_t, gdn_out_norm_g[j], batch, seq)
            wo = gdn_w_o[j]
        xs = _post(o, xs, mod[i], sandwich_g[i], wo.astype(BF16), ffn_w1[i].astype(BF16),
                   ffn_w2[i].astype(BF16), seq)
    return xs.reshape(batch, seq, d)
```

```python
import functools
import math

import jax
import jax.numpy as jnp
from jax import lax
from jax.experimental import pallas as pl
from jax.experimental.pallas import tpu as pltpu

F32 = jnp.float32
BF16 = jnp.bfloat16

LANES = 128
CHUNK = 64
HEADS = 8
HEAD_DIM = 128
ROPE_DIM = 64
NORM_EPS = 1e-6
ROPE_THETA = 10000.0
CONV_TAPS = 4
VMEM_LIMIT = 56 * 1024 * 1024

TOKEN_TILE = 512
ATTN_TILE = 2048
ATTN_ROWS = 256
ATTN_FULL_ROWS = 512
GDN_SEG = 256
POST_ROWS = 256
PRE_ROWS = 256


def _rms(x):
    return x * lax.rsqrt(jnp.mean(x * x, axis=-1, keepdims=True) + NORM_EPS)


def _dot(a, b):
    return jnp.dot(a, b, preferred_element_type=F32)


def _dot_nt(a, b):
    return lax.dot_general(a, b, (((1,), (1,)), ((), ())), preferred_element_type=F32)


def _dot_tn(a, b):
    return lax.dot_general(a, b, (((0,), (0,)), ((), ())), preferred_element_type=F32)


def _params(*sem):
    return pltpu.CompilerParams(dimension_semantics=sem, vmem_limit_bytes=VMEM_LIMIT)


def _resident(shape):
    return pl.BlockSpec(shape, lambda *_: (0,) * len(shape), pipeline_mode=pl.Buffered(1))


def _adaln_kernel(c_ref, w_ref, b_ref, o_ref):
    c = c_ref[...]
    act = (c * jax.nn.sigmoid(c)).astype(BF16)
    o_ref[0] = _dot(act, w_ref[0].astype(BF16)) + b_ref[0]


def _adaln(c, ada_w, ada_b):
    depth, d, n = ada_w.shape
    b = c.shape[0]
    tn = n // 4
    return pl.pallas_call(
        _adaln_kernel,
        out_shape=jax.ShapeDtypeStruct((depth, b, n), F32),
        grid=(depth, n // tn),
        in_specs=[pl.BlockSpec((b, d), lambda i, j: (0, 0)),
                  pl.BlockSpec((1, d, tn), lambda i, j: (i, 0, j)),
                  pl.BlockSpec((1, 1, tn), lambda i, j: (i, 0, j))],
        out_specs=pl.BlockSpec((1, b, tn), lambda i, j: (i, 0, j)),
        compiler_params=_params("arbitrary", "arbitrary"),
        name="adaln_mod",
    )(c, ada_w, ada_b.reshape(depth, 1, n))


def _rope_kernel(pos_ref, inv_ref, cos_ref, sin_ref):
    ang = pos_ref[...].astype(F32) * inv_ref[...]
    cos_ref[...] = jnp.cos(ang)
    sin_ref[...] = jnp.sin(ang)


def _rope_tables(positions):
    t = positions.size
    half = ROPE_DIM // 2
    inv_freq = ROPE_THETA ** (-jnp.arange(half, dtype=F32) / half)
    inv_row = jnp.tile(inv_freq, LANES // half).reshape(1, LANES)
    tm = 1024
    return pl.pallas_call(
        _rope_kernel,
        out_shape=(jax.ShapeDtypeStruct((t, LANES), F32),) * 2,
        grid=(t // tm,),
        in_specs=[pl.BlockSpec((tm, 1), lambda i: (i, 0)),
                  pl.BlockSpec((1, LANES), lambda i: (0, 0))],
        out_specs=(pl.BlockSpec((tm, LANES), lambda i: (i, 0)),) * 2,
        compiler_params=_params("arbitrary"),
        name="rope_tables",
    )(positions.reshape(t, 1), inv_row)


def _mla_pre_kernel(x_ref, mod_ref, g_ref, cos_ref, sin_ref, win_ref, gq_ref, gkv_ref,
                    wuq_ref, wukv_ref, q_ref, k_ref, v_ref, *, dq, dkv, scale):
    d_nope = HEADS * HEAD_DIM
    mod = mod_ref[0]
    h = _rms(x_ref[...]) * g_ref[...] * (1.0 + mod[1:2]) + mod[0:1]
    proj = _dot(h.astype(BF16), win_ref[...])
    cos = cos_ref[...]
    sin = sin_ref[...]
    k_rope = (proj[:, dq + dkv:dq + dkv + LANES] * cos
              + proj[:, dq + dkv + LANES:dq + dkv + 2 * LANES] * sin).astype(BF16)
    qn = (_rms(proj[:, :dq]) * gq_ref[...]).astype(BF16)
    qf = _dot(qn, wuq_ref[...])
    cos_h = jnp.concatenate([cos] * HEADS, axis=1)
    sin_h = jnp.concatenate([sin] * HEADS, axis=1)
    q_nope = (qf[:, :d_nope] * scale).astype(BF16)
    q_rope = ((qf[:, d_nope:2 * d_nope] * cos_h + qf[:, 2 * d_nope:] * sin_h) * scale).astype(BF16)
    kvn = (_rms(proj[:, dq:dq + dkv]) * gkv_ref[...]).astype(BF16)
    kv = _dot(kvn, wukv_ref[...])
    k_nope = kv[:, :d_nope].astype(BF16)
    v = kv[:, d_nope:].astype(BF16)
    ones = jnp.ones((v.shape[0], HEAD_DIM), BF16)
    for hd in range(HEADS):
        lo = hd * HEAD_DIM
        q_ref[:, 2 * lo:2 * lo + HEAD_DIM] = q_nope[:, lo:lo + HEAD_DIM]
        q_ref[:, 2 * lo + HEAD_DIM:2 * lo + 2 * HEAD_DIM] = q_rope[:, lo:lo + HEAD_DIM]
        k_ref[:, 2 * lo:2 * lo + HEAD_DIM] = k_nope[:, lo:lo + HEAD_DIM]
        k_ref[:, 2 * lo + HEAD_DIM:2 * lo + 2 * HEAD_DIM] = k_rope
        v_ref[:, 2 * lo:2 * lo + HEAD_DIM] = v[:, lo:lo + HEAD_DIM]
        v_ref[:, 2 * lo + HEAD_DIM:2 * lo + 2 * HEAD_DIM] = ones


def _swap_rot(w):
    half = ROPE_DIM // 2
    return jnp.concatenate([-w[..., half:], w[..., :half]], axis=-1)


def _mla_weights(w_in, w_uq, w_ukv, dq, dkv):
    d = w_in.shape[0]
    wkr = w_in[:, dq + dkv:]
    wkr_sw = _swap_rot(wkr)
    win = jnp.concatenate([w_in[:, :dq + dkv], wkr, wkr, wkr_sw, wkr_sw], axis=1).astype(BF16)
    wq = w_uq.reshape(dq, HEADS, HEAD_DIM + ROPE_DIM)
    nope = wq[:, :, :HEAD_DIM].reshape(dq, HEADS * HEAD_DIM)
    rope = wq[:, :, HEAD_DIM:]
    pad = jnp.zeros((dq, HEADS, HEAD_DIM - ROPE_DIM), F32)
    rope_p = jnp.concatenate([rope, pad], axis=-1).reshape(dq, HEADS * HEAD_DIM)
    rope_sw = jnp.concatenate([_swap_rot(rope), pad], axis=-1).reshape(dq, HEADS * HEAD_DIM)
    wuq = jnp.concatenate([nope, rope_p, rope_sw], axis=1).astype(BF16)
    wkv = w_ukv.reshape(dkv, HEADS, 2, HEAD_DIM).transpose(0, 2, 1, 3).reshape(dkv, 2 * HEADS * HEAD_DIM)
    del d
    return win, wuq, wkv.astype(BF16)


def _mla_pre(x, mod, g, cos, sin, win, gq, gkv, wuq, wukv, seq):
    t, d = x.shape
    dq, dkv = gq.shape[-1], gkv.shape[-1]
    tm = TOKEN_TILE
    per_seq = seq // tm
    d_nope = HEADS * HEAD_DIM
    scale = float((HEAD_DIM + ROPE_DIM) ** -0.5 * math.log2(math.e))
    row = lambda i: (i, 0)
    return pl.pallas_call(
        functools.partial(_mla_pre_kernel, dq=dq, dkv=dkv, scale=scale),
        out_shape=(jax.ShapeDtypeStruct((t, 2 * d_nope), BF16),
                   jax.ShapeDtypeStruct((t, 2 * d_nope), BF16),
                   jax.ShapeDtypeStruct((t, 2 * d_nope), BF16)),
        grid=(t // tm,),
        in_specs=[pl.BlockSpec((tm, d), row),
                  pl.BlockSpec((1, 6, d), lambda i: (i // per_seq, 0, 0)),
                  _resident((1, d)),
                  pl.BlockSpec((tm, LANES), row),
                  pl.BlockSpec((tm, LANES), row),
                  _resident(win.shape), _resident((1, dq)), _resident((1, dkv)),
                  _resident(wuq.shape), _resident(wukv.shape)],
        out_specs=(pl.BlockSpec((tm, 2 * d_nope), row),
                   pl.BlockSpec((tm, 2 * d_nope), row),
                   pl.BlockSpec((tm, 2 * d_nope), row)),
        compiler_params=_params("arbitrary"),
        name="mla_pre",
    )(x, mod, g.reshape(1, d), cos, sin, win, gq.reshape(1, dq), gkv.reshape(1, dkv), wuq, wukv)


def _attn_kernel(q_ref, k_ref, v_ref, o_ref, m_ref, acc_ref, bias_ref, *, tile):
    i = pl.program_id(2)

    @pl.when((pl.program_id(0) == 0) & (pl.program_id(1) == 0) & (i == 0))
    def _():
        qc = lax.broadcasted_iota(jnp.int32, (ATTN_ROWS, ATTN_ROWS), 0) // CHUNK
        kc = lax.broadcasted_iota(jnp.int32, (ATTN_ROWS, ATTN_ROWS), 1) // CHUNK
        bias_ref[...] = jnp.where(kc <= qc, 0.0, -jnp.inf)

    def step(j, diagonal):
        start = pl.multiple_of(j * tile, tile)
        k = k_ref[pl.ds(start, tile), :]
        v = v_ref[pl.ds(start, tile), :]
        size = ATTN_ROWS if diagonal else ATTN_FULL_ROWS
        blocks = [slice(r * size, (r + 1) * size) for r in range(tile // size)]
        if diagonal:
            blocks = blocks[::-1]
        seen = [rows.stop if diagonal else tile for rows in blocks]
        scores = [_dot_nt(q_ref[rows, :], k[:n]) for rows, n in zip(blocks, seen)]
        for rows, n, s in zip(blocks, seen, scores):
            if diagonal:
                own = s[:, rows.start:] + bias_ref[...]
                s = own if rows.start == 0 else jnp.concatenate([s[:, :rows.start], own], axis=1)
                m_new = jnp.max(s, axis=-1, keepdims=True)
                acc_ref[rows, :] = _dot(jnp.exp2(s - m_new).astype(BF16), v[:n])
            else:
                m_prev = m_ref[rows, :]
                m_new = jnp.maximum(m_prev, jnp.max(s, axis=-1, keepdims=True))
                p = jnp.exp2(s - m_new).astype(BF16)
                acc_ref[rows, :] = jnp.exp2(m_prev - m_new) * acc_ref[rows, :] + _dot(p, v[:n])
            m_ref[rows, :] = m_new

    def body(j, carry):
        step(j, False)
        return carry

    step(i, True)
    lax.fori_loop(0, i, body, 0)
    acc = acc_ref[...]
    o_ref[...] = (acc[:, :HEAD_DIM] / acc[:, HEAD_DIM:]).astype(o_ref.dtype)


def _attention(q, k, v, batch, seq):
    t = q.shape[0]
    tile = ATTN_TILE
    nq = seq // tile
    return pl.pallas_call(
        functools.partial(_attn_kernel, tile=tile),
        out_shape=jax.ShapeDtypeStruct((t, HEADS * HEAD_DIM), BF16),
        grid=(batch, HEADS, nq),
        in_specs=[pl.BlockSpec((tile, 2 * HEAD_DIM), lambda b, h, i: (b * nq + i, h)),
                  pl.BlockSpec((seq, 2 * HEAD_DIM), lambda b, h, i: (b, h)),
                  pl.BlockSpec((seq, 2 * HEAD_DIM), lambda b, h, i: (b, h))],
        out_specs=pl.BlockSpec((tile, HEAD_DIM), lambda b, h, i: (b * nq + i, h)),
        scratch_shapes=[pltpu.VMEM((tile, 1), F32), pltpu.VMEM((tile, 2 * HEAD_DIM), F32),
                        pltpu.VMEM((ATTN_ROWS, ATTN_ROWS), F32)],
        compiler_params=_params("arbitrary", "arbitrary", "arbitrary"),
        name="mla_attention",
    )(q, k, v)


def _post_kernel(o_ref, x_ref, mod_ref, g_ref, wo_ref, w1_ref, w2_ref, out_ref, h1_ref, *, ff_tile):
    mod = mod_ref[0]
    g = g_ref[...]
    d_ff = w1_ref.shape[1]
    blocks = [slice(r * POST_ROWS, (r + 1) * POST_ROWS) for r in range(x_ref.shape[0] // POST_ROWS)]
    ys = [_dot(o_ref[rows, :], wo_ref[...]) for rows in blocks]
    x1s = []
    for rows, y in zip(blocks, ys):
        x1 = x_ref[rows, :] + mod[2:3] * (_rms(y) * g[1:2])
        x1s.append(x1)
        h = (_rms(x1) * g[2:3] * (1.0 + mod[4:5]) + mod[3:4]).astype(BF16)
        for j in range(d_ff // ff_tile):
            a = jnp.maximum(_dot(h, w1_ref[:, j * ff_tile:(j + 1) * ff_tile]), 0.0)
            h1_ref[rows, j * ff_tile:(j + 1) * ff_tile] = (a * a).astype(BF16)
    for rows, x1 in zip(blocks, x1s):
        y2 = _dot(h1_ref[rows, :], w2_ref[...])
        out_ref[rows, :] = x1 + mod[5:6] * (_rms(y2) * g[3:4])


def _post(o, x, mod, g4, wo, w1, w2, seq):
    t, d = x.shape
    tm = TOKEN_TILE
    per_seq = seq // tm
    d_ff = w1.shape[1]
    row = lambda i: (i, 0)
    return pl.pallas_call(
        functools.partial(_post_kernel, ff_tile=1024),
        out_shape=jax.ShapeDtypeStruct((t, d), F32),
        grid=(t // tm,),
        in_specs=[pl.BlockSpec((tm, o.shape[1]), row),
                  pl.BlockSpec((tm, d), row),
                  pl.BlockSpec((1, 6, d), lambda i: (i // per_seq, 0, 0)),
                  _resident((4, d)),
                  _resident(wo.shape), _resident(w1.shape), _resident(w2.shape)],
        out_specs=pl.BlockSpec((tm, d), row),
        scratch_shapes=[pltpu.VMEM((tm, d_ff), BF16)],
        compiler_params=_params("arbitrary"),
        name="post_ffn",
    )(o, x, mod, g4, wo, w1, w2)


def _gdn_pre_kernel(x_ref, mod_ref, g_ref, win_ref, conv_ref, gate_ref,
                    q_ref, k_ref, v_ref, z_ref, gates_ref, buf_ref, *, per_seq):
    tm = x_ref.shape[0]
    d_qkv = 3 * HEADS * HEAD_DIM
    d_head = HEADS * HEAD_DIM
    halo = 8
    mod = mod_ref[0]
    gp = gate_ref[...]

    @pl.when(pl.program_id(0) % per_seq == 0)
    def _():
        buf_ref[0:halo, :] = jnp.zeros((halo, d_qkv), F32)

    blocks = [slice(r * PRE_ROWS, (r + 1) * PRE_ROWS) for r in range(tm // PRE_ROWS)]
    projs = []
    for rows in blocks:
        h = _rms(x_ref[rows, :]) * g_ref[...] * (1.0 + mod[1:2]) + mod[0:1]
        projs.append(_dot(h.astype(BF16), win_ref[...]))
    for rows, proj in zip(blocks, projs):
        buf_ref[halo + rows.start:halo + rows.stop, :] = proj[:, :d_qkv]
    last_rows = buf_ref[tm:tm + halo, :]

    for rows, proj in zip(blocks, projs):
        ext = buf_ref[rows.start:rows.start + halo + PRE_ROWS, :]
        conv = conv_ref[CONV_TAPS - 1:CONV_TAPS, :] * ext[halo:]
        for tap in range(CONV_TAPS - 1):
            conv = conv + conv_ref[tap:tap + 1, :] * pltpu.roll(ext, CONV_TAPS - 1 - tap, 0)[halo:]
        act = conv * jax.nn.sigmoid(conv)

        for hd in range(HEADS):
            lo = hd * HEAD_DIM
            qh = act[:, lo:lo + HEAD_DIM]
            kh = act[:, d_head + lo:d_head + lo + HEAD_DIM]
            qn = qh * lax.rsqrt(jnp.sum(qh * qh, axis=-1, keepdims=True) + 1e-6)
            q_ref[rows, lo:lo + HEAD_DIM] = (qn * (HEAD_DIM ** -0.5)).astype(BF16)
            k_ref[rows, lo:lo + HEAD_DIM] = (
                kh * lax.rsqrt(jnp.sum(kh * kh, axis=-1, keepdims=True) + 1e-6)).astype(BF16)
        v_ref[rows, :] = act[:, 2 * d_head:].astype(BF16)
        z = proj[:, d_qkv:d_qkv + d_head]
        z_ref[rows, :] = (z * jax.nn.sigmoid(z)).astype(BF16)

        raw = proj[:, d_qkv + d_head:]
        beta = jax.nn.sigmoid(raw)
        gl = gp[0:1] * jax.nn.softplus(raw + gp[1:2])
        pos = lax.broadcasted_iota(jnp.int32, (PRE_ROWS, LANES), 0) % CHUNK
        shift = 1
        while shift < CHUNK:
            gl = gl + jnp.where(pos >= shift, pltpu.roll(gl, shift, 0), 0.0)
            shift *= 2
        lane = lax.broadcasted_iota(jnp.int32, (PRE_ROWS, LANES), 1)
        gates_ref[rows, :] = jnp.where(lane < HEADS, beta, gl)
    buf_ref[0:halo, :] = last_rows


def _gdn_weights(w_in, a_log, dt_bias):
    d_head = HEADS * HEAD_DIM
    d_qkv = 3 * d_head
    pad = jnp.zeros((w_in.shape[0], LANES - 2 * HEADS), F32)
    win = jnp.concatenate([w_in[:, :d_qkv], w_in[:, d_qkv + 2 * HEADS:],
                           w_in[:, d_qkv:d_qkv + 2 * HEADS], pad], axis=1).astype(BF16)
    zeros = jnp.zeros((HEADS,), F32)
    lane_pad = jnp.zeros((LANES - 2 * HEADS,), F32)
    neg_a = jnp.concatenate([zeros, -jnp.exp(a_log.astype(F32)), lane_pad])
    dtb = jnp.concatenate([zeros, dt_bias.astype(F32), lane_pad])
    return win, jnp.stack([neg_a, dtb])


def _gdn_pre(x, mod, g, win, conv_w, gate_p, seq):
    t, d = x.shape
    tm = TOKEN_TILE
    per_seq = seq // tm
    d_head = HEADS * HEAD_DIM
    row = lambda i: (i, 0)
    tok = jax.ShapeDtypeStruct((t, d_head), BF16)
    return pl.pallas_call(
        functools.partial(_gdn_pre_kernel, per_seq=per_seq),
        out_shape=(tok, tok, tok, tok, jax.ShapeDtypeStruct((t, LANES), F32)),
        grid=(t // tm,),
        in_specs=[pl.BlockSpec((tm, d), row),
                  pl.BlockSpec((1, 6, d), lambda i: (i // per_seq, 0, 0)),
                  _resident((1, d)), _resident(win.shape), _resident(conv_w.shape),
                  _resident((2, LANES))],
        out_specs=(pl.BlockSpec((tm, d_head), row),) * 4 + (pl.BlockSpec((tm, LANES), row),),
        scratch_shapes=[pltpu.VMEM((tm + 8, 3 * d_head), F32)],
        compiler_params=_params("arbitrary"),
        name="gdn_pre",
    )(x, mod, g.reshape(1, d), win, conv_w, gate_p)


def _gdn_core_kernel(q_ref, k_ref, v_ref, z_ref, gates_ref, gt_ref, gout_ref, o_ref, state_ref):
    seg = q_ref.shape[0]
    n_chunks = seg // CHUNK
    n_tiles = seg // LANES
    heads = range(HEADS)

    @pl.when(pl.program_id(1) == 0)
    def _():
        state_ref[...] = jnp.zeros(state_ref.shape, F32)

    row = lax.broadcasted_iota(jnp.int32, (seg, seg), 0)
    col = lax.broadcasted_iota(jnp.int32, (seg, seg), 1)
    same = (row // CHUNK) == (col // CHUNK)
    lower = jnp.where(same & (row >= col), 1.0, 0.0)
    strict = jnp.where(same & (row > col), 1.0, 0.0)
    blocks = jnp.where(same, 1.0, 0.0)
    gout = gout_ref[...]

    t_row = lax.broadcasted_iota(jnp.int32, (seg, LANES), 0)
    t_lane = lax.broadcasted_iota(jnp.int32, (seg, LANES), 1)
    t_own = (t_lane // CHUNK) == ((t_row // CHUNK) % 2)
    t_i, t_j = t_row % CHUNK, t_lane % CHUNK
    w_i = lax.broadcasted_iota(jnp.int32, (CHUNK, seg), 0)
    w_lane = lax.broadcasted_iota(jnp.int32, (CHUNK, seg), 1)
    w_chunk, w_j = w_lane // CHUNK, w_lane % CHUNK
    lane_chunk = lax.broadcasted_iota(jnp.int32, (1, seg), 1) // CHUNK

    def tall(full):
        t = full[:, :LANES]
        for i in range(1, n_tiles):
            t = jnp.where(t_row // LANES == i, full[:, i * LANES:(i + 1) * LANES], t)
        return jnp.where(t_own, t, 0.0)

    def wide(full):
        w = full[:CHUNK]
        for c in range(1, n_chunks):
            w = jnp.where(w_chunk == c, full[c * CHUNK:(c + 1) * CHUNK], w)
        return w

    def left(t):
        return t.astype(BF16)

    def right(w):
        w = w.astype(BF16)
        return jnp.concatenate([w, w], axis=0)

    cols = [slice(hd * HEAD_DIM, (hd + 1) * HEAD_DIM) for hd in heads]
    g_row, low, a_qk, rhs, q_dec, kd_t = [], [], [], [], [], []
    for hd in heads:
        k_b = k_ref[:, cols[hd]]
        q_b = q_ref[:, cols[hd]]
        kf = k_b.astype(F32)
        beta = gates_ref[:, hd:hd + 1]
        g_col = gates_ref[:, HEADS + hd:HEADS + hd + 1]
        g_row.append(gt_ref[HEADS + hd:HEADS + hd + 1, :])
        decay = jnp.exp(jnp.minimum(g_col - g_row[hd], 0.0))
        kb = kf * beta
        low.append(_dot_nt(kb.astype(BF16), k_b) * (decay * strict))
        a_qk.append((_dot_nt(q_b, k_b) * (decay * lower)).astype(BF16))
        e_col = jnp.exp(g_col)
        rhs.append(jnp.concatenate([v_ref[:, cols[hd]].astype(F32) * beta, kb * e_col], axis=1))
        q_dec.append(q_b.astype(F32) * e_col)
        g_end = g_row[hd][:, CHUNK - 1:CHUNK]
        for c in range(1, n_chunks):
            g_end = jnp.where(lane_chunk == c, g_row[hd][:, (c + 1) * CHUNK - 1:(c + 1) * CHUNK], g_end)
        kd_t.append((kf.T * jnp.exp(g_end - g_row[hd])).astype(BF16))

    low_t = [tall(low[hd]) for hd in heads]
    inv_t = [jnp.where(t_own & (t_i == t_j), 1.0, 0.0) - jnp.where(t_i // 2 == t_j // 2, low_t[hd], 0.0)
             for hd in heads]
    inv_w = [jnp.where(w_i == w_j, 1.0, 0.0) - jnp.where(w_i // 2 == w_j // 2, wide(low[hd]), 0.0)
             for hd in heads]
    size = 2
    while size < CHUNK:
        joins = (t_i // (2 * size) == t_j // (2 * size)) & (t_i // size != t_j // size)
        step = [_dot(left(jnp.where(joins, low_t[hd], 0.0)), right(inv_w[hd])) for hd in heads]
        step = [_dot(left(inv_t[hd]), right(wide(step[hd]))) for hd in heads]
        inv_t = [inv_t[hd] - tall(step[hd]) for hd in heads]
        if 2 * size < CHUNK:
            inv_w = [inv_w[hd] - wide(step[hd]) for hd in heads]
        size *= 2
    inv = [(jnp.concatenate([inv_t[hd]] * n_tiles, axis=1) * blocks).astype(BF16) for hd in heads]
    uw = [_dot(inv[hd], rhs[hd].astype(BF16)) for hd in heads]

    state = [state_ref[hd] for hd in heads]
    zero_rows = jnp.zeros((CHUNK, HEAD_DIM), BF16)
    for n in range(n_chunks):
        rows = slice(n * CHUNK, (n + 1) * CHUNK)
        tile = slice((n // 2) * LANES, (n // 2 + 1) * LANES)
        ws, mix = [], []
        for hd in heads:
            lhs = jnp.concatenate([uw[hd][rows, HEAD_DIM:], q_dec[hd][rows]], axis=0).astype(BF16)
            ws.append(_dot(lhs, state[hd].astype(BF16)))
        for hd in heads:
            v_new = (uw[hd][rows, :HEAD_DIM] - ws[hd][:CHUNK]).astype(BF16)
            v_two = jnp.concatenate([v_new, zero_rows] if n % 2 == 0 else [zero_rows, v_new], axis=0)
            lhs = jnp.concatenate([a_qk[hd][rows, tile], kd_t[hd][:, tile]], axis=0)
            mix.append(_dot(lhs, v_two))
        for hd in heads:
            g_last = g_row[hd][:, (n + 1) * CHUNK - 1:(n + 1) * CHUNK]
            state[hd] = state[hd] * jnp.exp(g_last) + mix[hd][CHUNK:]
            o = _rms(ws[hd][CHUNK:] + mix[hd][:CHUNK]) * gout * z_ref[rows, cols[hd]].astype(F32)
            o_ref[rows, cols[hd]] = o.astype(o_ref.dtype)
    for hd in heads:
        state_ref[hd] = state[hd]


def _gdn_core(q, k, v, z, gates, gates_t, gout, batch, seq):
    t, d_head = q.shape
    seg = GDN_SEG
    per_seq = seq // seg
    blk = pl.BlockSpec((seg, d_head), lambda b, s: (b * per_seq + s, 0))
    return pl.pallas_call(
        _gdn_core_kernel,
        out_shape=jax.ShapeDtypeStruct((t, d_head), BF16),
        grid=(batch, per_seq),
        in_specs=[blk, blk, blk, blk,
                  pl.BlockSpec((seg, LANES), lambda b, s: (b * per_seq + s, 0)),
                  pl.BlockSpec((2 * HEADS, seg), lambda b, s: (0, b * per_seq + s)),
                  pl.BlockSpec((1, HEAD_DIM), lambda b, s: (0, 0))],
        out_specs=blk,
        scratch_shapes=[pltpu.VMEM((HEADS, HEAD_DIM, HEAD_DIM), F32)],
        compiler_params=_params("arbitrary", "arbitrary"),
        name="gdn_core",
    )(q, k, v, z, gates, gates_t, gout.reshape(1, HEAD_DIM))


def kernel(x, c, positions, ada_w, ada_b, sandwich_g, mla_w_in, mla_q_norm_g, mla_kv_norm_g, mla_w_uq, mla_w_ukv, mla_w_o, gdn_w_in, gdn_conv_w, gdn_a_log, gdn_dt_bias, gdn_out_norm_g, gdn_w_o, ffn_w1, ffn_w2):
    batch, seq, d = x.shape
    depth = ada_w.shape[0]
    t = batch * seq
    assert seq % TOKEN_TILE == 0 and seq % ATTN_TILE == 0 and seq % GDN_SEG == 0
    mod = _adaln(c, ada_w, ada_b).reshape(depth, batch, 6, d)
    cos, sin = _rope_tables(positions)
    xs = x.reshape(t, d)
    for i in range(depth):
        j = i // 2
        if i % 2 == 0:
            dq, dkv = mla_q_norm_g.shape[-1], mla_kv_norm_g.shape[-1]
            win, wuq, wukv = _mla_weights(mla_w_in[j], mla_w_uq[j], mla_w_ukv[j], dq, dkv)
            q, k, v = _mla_pre(xs, mod[i], sandwich_g[i, 0], cos, sin, win, mla_q_norm_g[j],
                               mla_kv_norm_g[j], wuq, wukv, seq)
            o = _attention(q, k, v, batch, seq)
            wo = mla_w_o[j]
        else:
            win, gate_p = _gdn_weights(gdn_w_in[j], gdn_a_log[j], gdn_dt_bias[j])
            q, k, v, z, gates = _gdn_pre(xs, mod[i], sandwich_g[i, 0], win, gdn_conv_w[j], gate_p, seq)
            gates_t = gates[:, :2 * HEADS].T
            o = _gdn_core(q, k, v, z, gates, gates_t, gdn_out_norm_g[j], batch, seq)
            wo = gdn_w_o[j]
        xs = _post(o, xs, mod[i], sandwich_g[i], wo.astype(BF16), ffn_w1[i].astype(BF16),
                   ffn_w2[i].astype(BF16), seq)
    return xs.reshape(batch, seq, d)
```
